```python
import math
import jax, jax.numpy as jnp
from jax import lax
import numpy as np

D_MODEL = 1024
BATCH = 32
SEQ = 2048
DEPTH = 2

CTX_LEN = 256
GRID_W = 64
RMS_EPS = 1e-6

D_HY = 512
HY_ORDER = 2
HY_DIRS = 2
HY_SHORT = 3
HY_BANDS = 8
HY_EMB = 1 + 2 * HY_BANDS
HY_FH = 64
HY_DECAY_TARGET = 1e-2
HY_FAST_DECAY = 0.3
HY_SLOW_DECAY = 1.5
HY_FILTER_SCALE = 0.02

GLA_HEADS = 4
GLA_DK = 64
GLA_DV = 128
D_GLA_K = GLA_HEADS * GLA_DK
D_GLA_V = GLA_HEADS * GLA_DV
GLA_GATE_RANK = 16
GLA_GATE_NORM = 16.0
GLA_CHUNK = 64

POOL_WINDOWS = (2, 4, 8, 16)
POOL_GROUPS = 4
POOL_GROUP = 128
D_POOL = POOL_GROUPS * POOL_GROUP

D_FF = 4 * D_MODEL
N_BRANCH = 3

HY_COLS = 3 * D_HY
GLA_COLS = 2 * D_GLA_K + 2 * D_GLA_V + 2 * GLA_GATE_RANK
POOL_COLS = D_POOL
GATE_COLS = N_BRANCH * D_MODEL
GLA_OFF = HY_COLS
POOL_OFF = GLA_OFF + GLA_COLS
GATE_OFF = POOL_OFF + POOL_COLS
N_IN = GATE_OFF + GATE_COLS

kernel_name = "hybrid_hyena_gla_pool_dit_trunk"


def rms_norm(x, g):
    xf = x.astype(jnp.float32)
    y = xf * lax.rsqrt(jnp.mean(xf * xf, axis=-1, keepdims=True) + RMS_EPS)
    return (y * g.astype(jnp.float32)).astype(x.dtype)


def ada_norm(x, g, shift, scale):
    return rms_norm(x, g) * (1.0 + scale) + shift


def short_conv(u, w, b):
    k = w.shape[0]
    pad = k // 2
    y = lax.conv_general_dilated(u, w[:, None, :].astype(u.dtype), window_strides=(1,),
                                 padding=((pad, k - 1 - pad),),
                                 dimension_numbers=("NWC", "WIO", "NWC"),
                                 feature_group_count=u.shape[-1])
    return y + b


def hyena_filters(L, w1, b1, freq, w2, b2, w3):
    f32 = jnp.float32
    t = jnp.arange(L, dtype=f32)[:, None]
    bands = jnp.arange(1, HY_BANDS + 1, dtype=f32)[None, :]
    ang = (2.0 * math.pi / L) * t * bands
    z = jnp.concatenate([t / L, jnp.cos(ang), jnp.sin(ang)], axis=-1)
    a = jnp.sin(freq * (z @ w1 + b1))
    a = jnp.sin(freq * (a @ w2 + b2))
    h = (a @ w3).astype(f32).reshape(L, HY_DIRS, HY_ORDER, D_HY)
    deltas = jnp.abs(jnp.linspace(math.log(HY_DECAY_TARGET) / HY_SLOW_DECAY,
                                  math.log(HY_DECAY_TARGET) / HY_FAST_DECAY, D_HY, dtype=f32))
    window = jnp.exp(-(t / L) * deltas[None, :])
    return h * window[:, None, None, :]


def bidir_long_conv(u, h_fwd, h_bwd, skip):
    L = u.shape[1]
    n = 2 * L
    k = jnp.concatenate([h_fwd, jnp.zeros_like(h_fwd[:1]), h_bwd[:0:-1]], axis=0)
    kf = jnp.fft.rfft(k, n=n, axis=0)
    uf32 = u.astype(jnp.float32)
    uf = jnp.fft.rfft(uf32, n=n, axis=1)
    y = jnp.fft.irfft(uf * kf[None], n=n, axis=1)[:, :L]
    return (y + uf32 * skip.astype(jnp.float32)).astype(u.dtype)


def hyena_branch(u, p):
    L = u.shape[1]
    u = short_conv(u, p["hy_short_w"], p["hy_short_b"])
    x1, x2, v = jnp.split(u, 3, axis=-1)
    h = hyena_filters(L, p["hy_f_w1"], p["hy_f_b1"], p["hy_f_freq"], p["hy_f_w2"], p["hy_f_b2"], p["hy_f_w3"])
    z = x1 * bidir_long_conv(v, h[:, 0, 0], h[:, 1, 0], p["hy_skip"][0])
    z = x2 * bidir_long_conv(z, h[:, 0, 1], h[:, 1, 1], p["hy_skip"][1])
    return z


def gla_chunked(q, k, v, log_a, s0):
    B, L, H, DK = q.shape
    DV = v.shape[-1]
    C = GLA_CHUNK
    N = L // C
    q = q.reshape(B, N, C, H, DK)
    k = k.reshape(B, N, C, H, DK)
    v = v.reshape(B, N, C, H, DV)
    b = jnp.cumsum(log_a.reshape(B, N, C, H, DK), axis=2)
    b_last = b[:, :, -1:]
    q_dec = q * jnp.exp(b)
    k_inv = k * jnp.exp(-b)
    k_end = k * jnp.exp(b_last - b)
    tri = jnp.tril(jnp.ones((C, C), jnp.float32))
    scores = jnp.einsum("bnihd,bnjhd->bnhij", q_dec, k_inv) * tri
    o = jnp.einsum("bnhij,bnjhv->bnihv", scores, v)
    kv = jnp.einsum("bnjhd,bnjhv->bnhdv", k_end, v)
    decay = jnp.exp(b_last[:, :, 0])

    def step(s, inp):
        dec, kv_n = inp
        return dec[..., None] * s + kv_n, s

    s_fin, s_prev = lax.scan(step, s0, (jnp.moveaxis(decay, 1, 0), jnp.moveaxis(kv, 1, 0)))
    o = o + jnp.einsum("bnihd,nbhdv->bnihv", q_dec, s_prev)
    return o.reshape(B, L, H, DV), s_fin


def gla_branch(u, p, s_f0, s_b0):
    f32 = jnp.float32
    B, L, _ = u.shape
    idx = (D_GLA_K, 2 * D_GLA_K, 2 * D_GLA_K + D_GLA_V, 2 * D_GLA_K + 2 * D_GLA_V,
           2 * D_GLA_K + 2 * D_GLA_V + GLA_GATE_RANK)
    q, k, v, r, gf, gb = jnp.split(u, idx, axis=-1)
    q = q.reshape(B, L, GLA_HEADS, GLA_DK).astype(f32) * (GLA_DK ** -0.5)
    k = k.reshape(B, L, GLA_HEADS, GLA_DK).astype(f32)
    v = v.reshape(B, L, GLA_HEADS, GLA_DV).astype(f32)
    la_f = (jax.nn.log_sigmoid((gf @ p["gla_wa_f"] + p["gla_ba_f"]).astype(f32)) / GLA_GATE_NORM
            ).reshape(B, L, GLA_HEADS, GLA_DK)
    la_b = (jax.nn.log_sigmoid((gb @ p["gla_wa_b"] + p["gla_ba_b"]).astype(f32)) / GLA_GATE_NORM
            ).reshape(B, L, GLA_HEADS, GLA_DK)
    o_f, s_f = gla_chunked(q, k, v, la_f, s_f0)
    o_b, s_b = gla_chunked(q[:, ::-1], k[:, ::-1], v[:, ::-1], la_b[:, ::-1], s_b0)
    o = o_f + o_b[:, ::-1]
    o = o * lax.rsqrt(jnp.mean(o * o, axis=-1, keepdims=True) + RMS_EPS) * p["gla_norm_w"].astype(f32)
    y = o.reshape(B, L, D_GLA_V).astype(u.dtype) * jax.nn.silu(r)
    return y, s_f, s_b


def window_bounds(n, w):
    t = jnp.arange(n)
    return jnp.clip(t - w // 2, 0, n), jnp.clip(t - w // 2 + w, 0, n)


def mean_pool_1d(u, w):
    L = u.shape[1]
    cs = jnp.pad(jnp.cumsum(u, axis=1), ((0, 0), (1, 0), (0, 0)))
    lo, hi = window_bounds(L, w)
    s = jnp.take(cs, hi, axis=1) - jnp.take(cs, lo, axis=1)
    return s / (hi - lo).astype(u.dtype)[None, :, None]


def mean_pool_2d(u, w):
    R, W = u.shape[1], u.shape[2]
    sat = jnp.pad(jnp.cumsum(jnp.cumsum(u, axis=1), axis=2), ((0, 0), (1, 0), (1, 0), (0, 0)))
    rl, rh = window_bounds(R, w)
    cl, ch = window_bounds(W, w)
    top = jnp.take(sat, rl, axis=1)
    bot = jnp.take(sat, rh, axis=1)
    s = (jnp.take(bot, ch, axis=2) - jnp.take(bot, cl, axis=2)
         - jnp.take(top, ch, axis=2) + jnp.take(top, cl, axis=2))
    cnt = ((rh - rl)[:, None] * (ch - cl)[None, :]).astype(u.dtype)
    return s / cnt[None, :, :, None]


def pool_branch(u, p, grid):
    B, L, _ = u.shape
    uf = u.astype(jnp.float32)
    outs = []
    for g, w in enumerate(POOL_WINDOWS):
        ug = uf[..., g * POOL_GROUP:(g + 1) * POOL_GROUP]
        if grid:
            rows = L // GRID_W
            m = mean_pool_2d(ug.reshape(B, rows, GRID_W, POOL_GROUP), w).reshape(B, L, POOL_GROUP)
        else:
            m = mean_pool_1d(ug, w)
        outs.append(m - ug)
    y = jnp.stack(outs, axis=2).astype(u.dtype)
    y = jnp.einsum("blgc,gcd->blgd", y, p["pool_w"]).reshape(B, L, D_POOL)
    return y * p["pool_scale"]


def token_mixers(proj, p, grid, s_f0, s_b0):
    y_hy = hyena_branch(proj[..., :GLA_OFF], p)
    y_gla, s_f, s_b = gla_branch(proj[..., GLA_OFF:POOL_OFF], p, s_f0, s_b0)
    y_pool = pool_branch(proj[..., POOL_OFF:GATE_OFF], p, grid)
    g_hy, g_gla, g_pool = jnp.split(jax.nn.sigmoid(proj[..., GATE_OFF:]), N_BRANCH, axis=-1)
    merged = (g_hy * (y_hy @ p["w_br_hy"]) + g_gla * (y_gla @ p["w_br_gla"])
              + g_pool * (y_pool @ p["w_br_pool"]))
    return merged @ p["w_out"], s_f, s_b


def sqrelu_mlp(h, w_up, w_down):
    return jnp.square(jax.nn.relu(h @ w_up)) @ w_down


def setup_inputs(seed: int = 0) -> dict:
    key = jax.random.key(seed)
    ks = jax.random.split(key, 40)
    f32 = jnp.float32

    def nrm(k, shape, scale):
        return jax.random.normal(k, shape, f32) * scale

    return {
        "x": nrm(ks[0], (BATCH, SEQ, D_MODEL), 1.0),
        "c": nrm(ks[1], (BATCH, D_MODEL), 1.0),
        "ctx": nrm(ks[2], (BATCH, CTX_LEN, D_MODEL), 1.0),
        "c_ctx": nrm(ks[3], (D_MODEL,), 1.0),
        "w_mod": nrm(ks[4], (DEPTH, D_MODEL, 6 * D_MODEL), D_MODEL ** -0.5),
        "b_mod": nrm(ks[5], (DEPTH, 6 * D_MODEL), 0.02),
        "norm_mix": 1.0 + nrm(ks[6], (DEPTH, D_MODEL), 0.02),
        "norm_ffn": 1.0 + nrm(ks[7], (DEPTH, D_MODEL), 0.02),
        "w_in": nrm(ks[8], (DEPTH, D_MODEL, N_IN), D_MODEL ** -0.5),
        "b_in": nrm(ks[9], (DEPTH, N_IN), 0.02),
        "hy_short_w": nrm(ks[10], (DEPTH, HY_SHORT, HY_COLS), HY_SHORT ** -0.5),
        "hy_short_b": nrm(ks[11], (DEPTH, HY_COLS), 0.02),
        "hy_f_w1": nrm(ks[12], (DEPTH, HY_EMB, HY_FH), HY_EMB ** -0.5),
        "hy_f_b1": nrm(ks[13], (DEPTH, HY_FH), 0.02),
        "hy_f_freq": 1.0 + nrm(ks[14], (DEPTH, HY_FH), 0.02),
        "hy_f_w2": nrm(ks[15], (DEPTH, HY_FH, HY_FH), HY_FH ** -0.5),
        "hy_f_b2": nrm(ks[16], (DEPTH, HY_FH), 0.02),
        "hy_f_w3": nrm(ks[17], (DEPTH, HY_FH, HY_DIRS * HY_ORDER * D_HY), HY_FILTER_SCALE),
        "hy_skip": nrm(ks[18], (DEPTH, HY_ORDER, D_HY), 0.5),
        "gla_wa_f": nrm(ks[19], (DEPTH, GLA_GATE_RANK, D_GLA_K), GLA_GATE_RANK ** -0.5),
        "gla_ba_f": nrm(ks[20], (DEPTH, D_GLA_K), 0.02),
        "gla_wa_b": nrm(ks[21], (DEPTH, GLA_GATE_RANK, D_GLA_K), GLA_GATE_RANK ** -0.5),
        "gla_ba_b": nrm(ks[22], (DEPTH, D_GLA_K), 0.02),
        "gla_norm_w": 1.0 + nrm(ks[23], (DEPTH, GLA_DV), 0.02),
        "pool_w": nrm(ks[24], (DEPTH, POOL_GROUPS, POOL_GROUP, POOL_GROUP), POOL_GROUP ** -0.5),
        "pool_scale": 1.0 + nrm(ks[25], (DEPTH, D_POOL), 0.02),
        "w_br_hy": nrm(ks[26], (DEPTH, D_HY, D_MODEL), D_HY ** -0.5),
        "w_br_gla": nrm(ks[27], (DEPTH, D_GLA_V, D_MODEL), D_GLA_V ** -0.5),
        "w_br_pool": nrm(ks[28], (DEPTH, D_POOL, D_MODEL), D_POOL ** -0.5),
        "w_out": nrm(ks[29], (DEPTH, D_MODEL, D_MODEL), D_MODEL ** -0.5),
        "w_up": nrm(ks[30], (DEPTH, D_MODEL, D_FF), D_MODEL ** -0.5),
        "w_down": nrm(ks[31], (DEPTH, D_FF, D_MODEL), D_FF ** -0.5),
        "norm_final": 1.0 + nrm(ks[32], (D_MODEL,), 0.02),
    }


def reference(x, c, ctx, c_ctx, w_mod, b_mod, norm_mix, norm_ffn, w_in, b_in,
              hy_short_w, hy_short_b, hy_f_w1, hy_f_b1, hy_f_freq, hy_f_w2, hy_f_b2, hy_f_w3, hy_skip,
              gla_wa_f, gla_ba_f, gla_wa_b, gla_ba_b, gla_norm_w,
              pool_w, pool_scale, w_br_hy, w_br_gla, w_br_pool, w_out, w_up, w_down, norm_final):
    B = x.shape[0]
    s_zero = jnp.zeros((B, GLA_HEADS, GLA_DK, GLA_DV), jnp.float32)
    for l in range(DEPTH):
        p = {
            "hy_short_w": hy_short_w[l], "hy_short_b": hy_short_b[l],
            "hy_f_w1": hy_f_w1[l], "hy_f_b1": hy_f_b1[l], "hy_f_freq": hy_f_freq[l],
            "hy_f_w2": hy_f_w2[l], "hy_f_b2": hy_f_b2[l], "hy_f_w3": hy_f_w3[l], "hy_skip": hy_skip[l],
            "gla_wa_f": gla_wa_f[l], "gla_ba_f": gla_ba_f[l], "gla_wa_b": gla_wa_b[l], "gla_ba_b": gla_ba_b[l],
            "gla_norm_w": gla_norm_w[l], "pool_w": pool_w[l], "pool_scale": pool_scale[l],
            "w_br_hy": w_br_hy[l], "w_br_gla": w_br_gla[l], "w_br_pool": w_br_pool[l], "w_out": w_out[l],
        }
        mod_x = (jax.nn.silu(c) @ w_mod[l] + b_mod[l])[:, None, :]
        mod_c = (jax.nn.silu(c_ctx) @ w_mod[l] + b_mod[l])[None, None, :]
        sh1x, sc1x, g1x, sh2x, sc2x, g2x = jnp.split(mod_x, 6, axis=-1)
        sh1c, sc1c, g1c, sh2c, sc2c, g2c = jnp.split(mod_c, 6, axis=-1)

        hc = ada_norm(ctx, norm_mix[l], sh1c, sc1c)
        if l == DEPTH - 1:
            pc = hc @ w_in[l][:, GLA_OFF:POOL_OFF] + b_in[l][GLA_OFF:POOL_OFF]
            _, s_f, s_b = gla_branch(pc, p, s_zero, s_zero)
        else:
            pc = hc @ w_in[l] + b_in[l]
            mix_c, s_f, s_b = token_mixers(pc, p, False, s_zero, s_zero)
            ctx = ctx + g1c * mix_c
            ctx = ctx + g2c * sqrelu_mlp(ada_norm(ctx, norm_ffn[l], sh2c, sc2c), w_up[l], w_down[l])

        hx = ada_norm(x, norm_mix[l], sh1x, sc1x)
        px = hx @ w_in[l] + b_in[l]
        mix_x, _, _ = token_mixers(px, p, True, s_f, s_b)
        x = x + g1x * mix_x
        x = x + g2x * sqrelu_mlp(ada_norm(x, norm_ffn[l], sh2x, sc2x), w_up[l], w_down[l])
    return rms_norm(x, norm_final)
```

```python
import functools
import math

import numpy as np
import jax
import jax.numpy as jnp
from jax import lax
from jax.experimental import pallas as pl
from jax.experimental.pallas import tpu as pltpu

F32 = jnp.float32
BF16 = jnp.bfloat16

D_MODEL = 1024
DEPTH = 2
GRID_W = 64
RMS_EPS = 1e-6

D_HY = 512
HY_BANDS = 8
HY_DECAY_TARGET = 1e-2
HY_FAST_DECAY = 0.3
HY_SLOW_DECAY = 1.5

GLA_HEADS = 4
GLA_DK = 64
GLA_DV = 128
D_GLA_K = GLA_HEADS * GLA_DK
D_GLA_V = GLA_HEADS * GLA_DV
GLA_GATE_RANK = 16
GLA_GATE_NORM = 16.0
GLA_CHUNK = 64
GLA_BLOCK = 256

POOL_WINDOWS = (2, 4, 8, 16)
POOL_GROUP = 128
D_POOL = 512
D_FF = 4 * D_MODEL

HY_COLS = 3 * D_HY
GLA_MAIN = 2 * D_GLA_K + 2 * D_GLA_V
GLA_OFF = HY_COLS
RANK_OFF = GLA_OFF + GLA_MAIN
POOL_OFF = RANK_OFF + 2 * GLA_GATE_RANK
GATE_OFF = POOL_OFF + D_POOL
N_IN = GATE_OFF + 3 * D_MODEL
RANK_PAD = 128

VMEM_LIMIT_V7X = 56 * 1024 * 1024
HY_CT = 256
FREQ_ROWS = 64
MOD_ROWS = 40


def _cparams(sem):
    return pltpu.CompilerParams(dimension_semantics=sem, vmem_limit_bytes=VMEM_LIMIT_V7X)


def _dot(a, b):
    return jnp.dot(a, b, preferred_element_type=F32)


def _dot_nt(a, b):
    return lax.dot_general(a, b, (((1,), (1,)), ((), ())), preferred_element_type=F32)


def _dot_tn(a, b):
    return lax.dot_general(a, b, (((0,), (0,)), ((), ())), preferred_element_type=F32)


def _split(a):
    hi = a.astype(BF16)
    lo = (a - hi.astype(F32)).astype(BF16)
    return hi, lo


def _resident(shape):
    nd = len(shape)
    return pl.BlockSpec(shape, lambda *_: (0,) * nd, pipeline_mode=pl.Buffered(1))


def _mod_kernel(c_ref, w_ref, b_ref, o_ref):
    c = c_ref[...]
    a = c * jax.nn.sigmoid(c)
    a_hi, a_lo = _split(a)
    w_hi, w_lo = _split(w_ref[0])
    o_ref[0] = _dot(a_hi, w_hi) + _dot(a_lo, w_hi) + _dot(a_hi, w_lo) + b_ref[0]


def _modulation(cc, w_mod, b_mod):
    tn = 1536
    n = w_mod.shape[-1]
    return pl.pallas_call(
        _mod_kernel,
        grid=(DEPTH, n // tn),
        in_specs=[
            pl.BlockSpec((MOD_ROWS, D_MODEL), lambda l, j: (0, 0)),
            pl.BlockSpec((1, D_MODEL, tn), lambda l, j: (l, 0, j)),
            pl.BlockSpec((1, 1, tn), lambda l, j: (l, 0, j)),
        ],
        out_specs=pl.BlockSpec((1, MOD_ROWS, tn), lambda l, j: (l, 0, j)),
        out_shape=jax.ShapeDtypeStruct((DEPTH, MOD_ROWS, n), F32),
        compiler_params=_cparams(("arbitrary", "arbitrary")),
        name="modulation",
    )(cc, w_mod, b_mod.reshape(DEPTH, 1, n))


def _ada_norm(x, g, shift, scale):
    ms = jnp.mean(x * x, axis=-1, keepdims=True)
    return x * lax.rsqrt(ms + RMS_EPS) * g * (1.0 + scale) + shift


def _in_kernel(x_ref, sh_ref, sc_ref, g_ref, *refs, segs):
    ns = len(segs)
    w_refs, b_refs, o_refs = refs[:ns], refs[ns:2 * ns], refs[2 * ns:]
    h = _ada_norm(x_ref[...], g_ref[...], sh_ref[0], sc_ref[0]).astype(BF16)
    for (width, _, sig), w_ref, b_ref, o_ref in zip(segs, w_refs, b_refs, o_refs):
        step = min(width, 768)
        for c0 in range(0, width, step):
            acc = _dot(h, w_ref[:, c0:c0 + step]) + b_ref[:, c0:c0 + step]
            if sig:
                acc = jax.nn.sigmoid(acc)
            o_ref[:, c0:c0 + step] = acc.astype(o_ref.dtype)


def _in_proj(x2d, shift, scale, g, weights, biases, segs, tokens_per_batch, tm):
    n_tok = x2d.shape[0]
    tpb = tokens_per_batch // tm
    in_specs = [
        pl.BlockSpec((tm, D_MODEL), lambda i: (i, 0)),
        pl.BlockSpec((1, 1, D_MODEL), lambda i: (i // tpb, 0, 0)),
        pl.BlockSpec((1, 1, D_MODEL), lambda i: (i // tpb, 0, 0)),
        _resident((1, D_MODEL)),
    ]
    in_specs += [_resident((D_MODEL, s[0])) for s in segs]
    in_specs += [_resident((1, s[0])) for s in segs]
    return pl.pallas_call(
        functools.partial(_in_kernel, segs=segs),
        grid=(n_tok // tm,),
        in_specs=in_specs,
        out_specs=[pl.BlockSpec((tm, s[0]), lambda i: (i, 0)) for s in segs],
        out_shape=[jax.ShapeDtypeStruct((n_tok, s[0]), s[1]) for s in segs],
        compiler_params=_cparams(("parallel",)),
        name="in_proj",
    )(x2d, shift, scale, g, *weights, *biases)


def _short_conv(u_ref, w_ref, b_ref):
    u = u_ref[0].astype(F32)
    n = u.shape[0]
    row = lax.broadcasted_iota(jnp.int32, u.shape, 0)
    prev = jnp.where(row == 0, 0.0, pltpu.roll(u, 1, 0))
    nxt = jnp.where(row == n - 1, 0.0, pltpu.roll(u, n - 1, 0))
    return w_ref[0:1, :] * prev + w_ref[1:2, :] * u + w_ref[2:3, :] * nxt + b_ref[...]


def _hy_kernel(x1_ref, x2_ref, v_ref, w1_ref, w2_ref, wv_ref, b1_ref, b2_ref, bv_ref, skip_ref,
               f_ref, g_ref, k_ref, o_ref, xs_ref, z_ref, uf_ref, yf_ref, *, seq, blk):
    nb = seq // blk
    xs_ref[0] = _short_conv(x1_ref, w1_ref, b1_ref)
    xs_ref[1] = _short_conv(x2_ref, w2_ref, b2_ref)
    xs_ref[2] = _short_conv(v_ref, wv_ref, bv_ref)

    def long_conv(read_u, gate_idx, order, write):
        for j in range(nb):
            uf_ref[j] = _dot(f_ref[...], read_u(j).astype(BF16))
        for i in range(nb):
            def freq_step(r, carry, i=i):
                rows = pl.ds(pl.multiple_of(r * FREQ_ROWS, FREQ_ROWS), FREQ_ROWS)
                rows_im = pl.ds(pl.multiple_of(blk + r * FREQ_ROWS, FREQ_ROWS), FREQ_ROWS)
                re = jnp.zeros((FREQ_ROWS, HY_CT), F32)
                im = jnp.zeros((FREQ_ROWS, HY_CT), F32)
                for j in range(nb):
                    lag = i - j + nb - 1
                    kre = k_ref[order, lag, 0, rows, :]
                    kim = k_ref[order, lag, 1, rows, :]
                    ure = uf_ref[j, rows, :]
                    uim = uf_ref[j, rows_im, :]
                    re = re + ure * kre - uim * kim
                    im = im + ure * kim + uim * kre
                yf_ref[rows, :] = re
                yf_ref[rows_im, :] = im
                return carry
            lax.fori_loop(0, blk // FREQ_ROWS, freq_step, 0)
            re0 = jnp.zeros((8, HY_CT), F32)
            im0 = jnp.zeros((8, HY_CT), F32)
            for j in range(nb):
                lag = i - j + nb - 1
                re0 = re0 + uf_ref[j, 0:8, :] * k_ref[order, lag, 0, 0:8, :]
                im0 = im0 + uf_ref[j, blk:blk + 8, :] * k_ref[order, lag, 1, 0:8, :]
            first = lax.broadcasted_iota(jnp.int32, (8, HY_CT), 0) == 0
            yf_ref[0:8, :] = jnp.where(first, re0, yf_ref[0:8, :])
            yf_ref[blk:blk + 8, :] = jnp.where(first, im0, yf_ref[blk:blk + 8, :])
            y = _dot(g_ref[...], yf_ref[...].astype(BF16))
            u = read_u(i)
            gate = xs_ref[gate_idx, i * blk:(i + 1) * blk, :]
            write(i, gate * (y + u * skip_ref[order:order + 1, :]))

    def write_z(i, val):
        z_ref[i * blk:(i + 1) * blk, :] = val

    def write_o(i, val):
        o_ref[0, i * blk:(i + 1) * blk, :] = val.astype(o_ref.dtype)

    long_conv(lambda j: xs_ref[2, j * blk:(j + 1) * blk, :], 0, 0, write_z)
    long_conv(lambda j: z_ref[j * blk:(j + 1) * blk, :], 1, 1, write_o)


def _hyena(p_hy, short_w, short_b, skip, fmat, gmat, kspec, blk):
    bsz, seq, _ = p_hy.shape
    nb = seq // blk
    nlag = 2 * nb - 1
    nct = D_HY // HY_CT

    def col(off):
        return pl.BlockSpec((1, seq, HY_CT), lambda c, b, off=off: (b, 0, off + c))

    def wcol(rows, off):
        return pl.BlockSpec((rows, HY_CT), lambda c, b, off=off: (0, off + c))

    in_specs = [col(0), col(nct), col(2 * nct),
                wcol(3, 0), wcol(3, nct), wcol(3, 2 * nct),
                wcol(1, 0), wcol(1, nct), wcol(1, 2 * nct),
                wcol(2, 0),
                _resident((2 * blk, blk)), _resident((blk, 2 * blk)),
                pl.BlockSpec((2, nlag, 2, blk, HY_CT), lambda c, b: (0, 0, 0, 0, c),
                             pipeline_mode=pl.Buffered(1))]
    return pl.pallas_call(
        functools.partial(_hy_kernel, seq=seq, blk=blk),
        grid=(nct, bsz),
        in_specs=in_specs,
        out_specs=pl.BlockSpec((1, seq, HY_CT), lambda c, b: (b, 0, c)),
        out_shape=jax.ShapeDtypeStruct((bsz, seq, D_HY), BF16),
        scratch_shapes=[pltpu.VMEM((3, seq, HY_CT), F32), pltpu.VMEM((seq, HY_CT), F32),
                        pltpu.VMEM((nb, 2 * blk, HY_CT), F32), pltpu.VMEM((2 * blk, HY_CT), F32)],
        compiler_params=_cparams(("arbitrary", "arbitrary")),
        name="hyena",
    )(p_hy, p_hy, p_hy, short_w, short_w, short_w, short_b, short_b, short_b, skip, fmat, gmat, kspec)


def _dft_mats(blk):
    n = 2 * blk
    f = lax.broadcasted_iota(jnp.int32, (blk, blk), 0)
    t = lax.broadcasted_iota(jnp.int32, (blk, blk), 1)
    ang = ((f * t) % n).astype(F32) * (2.0 * math.pi / n)
    alt = jnp.where(t % 2 == 0, 1.0, -1.0).astype(F32)
    fre = jnp.cos(ang)
    fim = jnp.where(f == 0, alt, -jnp.sin(ang))
    fmat = jnp.concatenate([fre, fim], axis=0).astype(BF16)
    gre = jnp.where(f == 0, 1.0 / n, (2.0 / n) * jnp.cos(ang)).T
    gim = jnp.where(f == 0, alt / n, -(2.0 / n) * jnp.sin(ang)).T
    gmat = jnp.concatenate([gre, gim], axis=1).astype(BF16)
    return fmat, gmat


def _hyena_filters(seq, w1, b1, freq, w2, b2, w3):
    hp = lax.Precision.HIGHEST
    t = jnp.arange(seq, dtype=F32)[:, None]
    bands = jnp.arange(1, HY_BANDS + 1, dtype=F32)[None, :]
    ang = (2.0 * math.pi / seq) * t * bands
    z = jnp.concatenate([t / seq, jnp.cos(ang), jnp.sin(ang)], axis=-1)
    a = jnp.sin(freq * (jnp.dot(z, w1, precision=hp) + b1))
    a = jnp.sin(freq * (jnp.dot(a, w2, precision=hp) + b2))
    h = jnp.dot(a, w3, precision=hp).astype(F32).reshape(seq, 2, 2, D_HY)
    deltas = jnp.abs(jnp.linspace(math.log(HY_DECAY_TARGET) / HY_SLOW_DECAY,
                                  math.log(HY_DECAY_TARGET) / HY_FAST_DECAY, D_HY, dtype=F32))
    window = jnp.exp(-(t / seq) * deltas[None, :])
    return h * window[:, None, None, :]


def _filter_spectra(h, seq, blk):
    nb = seq // blk
    out = []
    for order in range(2):
        hf, hb = h[:, 0, order], h[:, 1, order]
        kk = jnp.concatenate([jnp.zeros((1, D_HY), F32), hb[:0:-1], hf], axis=0)
        lags = []
        for m in range(-(nb - 1), nb):
            c0 = m * blk + seq
            seg = jnp.concatenate([kk[c0:c0 + blk], jnp.zeros((1, D_HY), F32), kk[c0 - blk + 1:c0]], axis=0)
            kf = jnp.fft.rfft(seg, axis=0)
            re = jnp.real(kf)[:blk]
            im = jnp.concatenate([jnp.real(kf)[blk:blk + 1], jnp.imag(kf)[1:blk]], axis=0)
            lags.append(jnp.stack([re, im], axis=0))
        out.append(jnp.stack(lags, axis=0))
    return jnp.stack(out, axis=0).astype(F32)


def _gla_kernel(qkvr_ref, g_ref, wa_ref, ba_ref, nw_ref, s0_ref, tri_ref, y_ref, s_ref, o_ref, st_ref, *, seq):
    nblk = seq // GLA_BLOCK
    lane = lax.broadcasted_iota(jnp.int32, (1, 2 * GLA_DK), 1)
    head_mask = [(lane < GLA_DK).astype(F32), (lane >= GLA_DK).astype(F32)]
    ri = lax.broadcasted_iota(jnp.int32, (GLA_BLOCK, GLA_BLOCK), 0)
    ci = lax.broadcasted_iota(jnp.int32, (GLA_BLOCK, GLA_BLOCK), 1)
    rchunk = lax.broadcasted_iota(jnp.int32, (GLA_BLOCK, 1), 0) // GLA_CHUNK
    nchunk = GLA_BLOCK // GLA_CHUNK

    for d in range(2):
        causal = (ci <= ri) if d == 0 else (ci >= ri)
        for h in range(GLA_HEADS):
            st_ref[h] = s0_ref[0, d, h]

        def block_step(it, carry, d=d, causal=causal):
            bi = it if d == 0 else nblk - 1 - it
            rows = pl.ds(pl.multiple_of(bi * GLA_BLOCK, GLA_BLOCK), GLA_BLOCK)
            g_hi, g_lo = _split(g_ref[0, rows, :])
            w_hi, w_lo = _split(wa_ref[d])
            logit = _dot(g_hi, w_hi) + _dot(g_lo, w_hi) + _dot(g_hi, w_lo) + ba_ref[d]
            la = (jnp.minimum(logit, 0.0) - jnp.log1p(jnp.exp(-jnp.abs(logit)))) * (1.0 / GLA_GATE_NORM)
            la_hi, la_lo = _split(la)
            cum_blk = _dot(tri_ref[d, 0], la_hi) + _dot(tri_ref[d, 0], la_lo)
            cum_loc = _dot(tri_ref[d, 1], la_hi) + _dot(tri_ref[d, 1], la_lo)
            before = cum_blk - cum_loc
            edge = GLA_BLOCK - 1 if d == 0 else 0
            total = cum_blk[edge:edge + 1, :]
            q = qkvr_ref[0, rows, 0:D_GLA_K].astype(F32) * (GLA_DK ** -0.5)
            k = qkvr_ref[0, rows, D_GLA_K:2 * D_GLA_K].astype(F32)
            q_loc = q * jnp.exp(cum_loc)
            q_int = (q_loc * jnp.exp(before)).astype(BF16)
            q_loc = q_loc.astype(BF16)
            k_st = (k * jnp.exp(total - cum_blk)).astype(BF16)
            k_strip = []
            for a in range(nchunk):
                r0 = a * GLA_CHUNK
                seen = (rchunk <= a) if d == 0 else (rchunk >= a)
                expo = jnp.where(seen, before[r0:r0 + 1, :] - cum_blk, 0.0)
                k_strip.append((k * jnp.exp(expo)).astype(BF16))
            dec = jnp.exp(total)
            for h in range(GLA_HEADS):
                pair = slice((h // 2) * 2 * GLA_DK, (h // 2 + 1) * 2 * GLA_DK)
                hm = head_mask[h % 2].astype(BF16)
                v_h = qkvr_ref[0, rows, 2 * D_GLA_K + h * GLA_DV:2 * D_GLA_K + (h + 1) * GLA_DV]
                strips = []
                for a in range(nchunk):
                    r0 = a * GLA_CHUNK
                    strips.append(_dot_nt(q_loc[r0:r0 + GLA_CHUNK, pair] * hm, k_strip[a][:, pair]))
                scores = jnp.where(causal, jnp.concatenate(strips, axis=0), 0.0).astype(BF16)
                st = st_ref[h]
                o_h = _dot(scores, v_h) + _dot_nt(q_int[:, pair] * hm, st.astype(BF16))
                st_ref[h] = dec[:, pair] * st + _dot_tn(v_h, k_st[:, pair] * hm)
                cols = slice(h * GLA_DV, (h + 1) * GLA_DV)
                if d == 0:
                    o_ref[rows, cols] = o_h
                else:
                    o_ref[rows, cols] = o_ref[rows, cols] + o_h
            return carry

        lax.fori_loop(0, nblk, block_step, 0)
        for h in range(GLA_HEADS):
            s_ref[0, d, h] = st_ref[h]

    for h in range(GLA_HEADS):
        cols = slice(h * GLA_DV, (h + 1) * GLA_DV)
        o = o_ref[:, cols]
        o = o * lax.rsqrt(jnp.mean(o * o, axis=-1, keepdims=True) + RMS_EPS) * nw_ref[...]
        r = qkvr_ref[0, :, 2 * D_GLA_K + D_GLA_V + h * GLA_DV:2 * D_GLA_K + D_GLA_V + (h + 1) * GLA_DV].astype(F32)
        y_ref[0, :, cols] = (o * (r * jax.nn.sigmoid(r))).astype(y_ref.dtype)


def _gla_consts():
    i = np.arange(GLA_BLOCK)
    same = (i[:, None] // GLA_CHUNK) == (i[None, :] // GLA_CHUNK)
    lower = i[None, :] <= i[:, None]
    tri = np.stack([np.stack([lower, lower & same]), np.stack([lower.T, lower.T & same])])
    return jnp.asarray(tri.astype(np.float32), dtype=BF16)


def _gla(qkvr, rank, wa_pad, ba, norm_w, s0):
    bsz, seq, _ = qkvr.shape
    st_shape = (bsz, 2, GLA_HEADS, GLA_DV, 2 * GLA_DK)
    return pl.pallas_call(
        functools.partial(_gla_kernel, seq=seq),
        grid=(bsz,),
        in_specs=[
            pl.BlockSpec((1, seq, GLA_MAIN), lambda b: (b, 0, 0)),
            pl.BlockSpec((1, seq, RANK_PAD), lambda b: (b, 0, 0)),
            _resident((2, RANK_PAD, D_GLA_K)),
            _resident((2, 1, D_GLA_K)),
            _resident((1, GLA_DV)),
            pl.BlockSpec((1,) + st_shape[1:], lambda b: (b, 0, 0, 0, 0)),
            _resident((2, 2, GLA_BLOCK, GLA_BLOCK)),
        ],
        out_specs=[pl.BlockSpec((1, seq, D_GLA_V), lambda b: (b, 0, 0)),
                   pl.BlockSpec((1,) + st_shape[1:], lambda b: (b, 0, 0, 0, 0))],
        out_shape=[jax.ShapeDtypeStruct((bsz, seq, D_GLA_V), BF16), jax.ShapeDtypeStruct(st_shape, F32)],
        scratch_shapes=[pltpu.VMEM((seq, D_GLA_V), F32), pltpu.VMEM((GLA_HEADS, GLA_DV, 2 * GLA_DK), F32)],
        compiler_params=_cparams(("parallel",)),
        name="gla",
    )(qkvr, rank, wa_pad, ba, norm_w, s0, _gla_consts())


def _pool_kernel(u_ref, a_ref, inv_ref, pw_ref, ps_ref, o_ref, *, seq, width):
    rows_n = seq // width
    nblk = seq // 256
    for g, w in enumerate(POOL_WINDOWS):
        cols = slice(g * POOL_GROUP, (g + 1) * POOL_GROUP)
        ug = u_ref[0, :, cols]
        col = jnp.concatenate([_dot(a_ref[g], ug[b * 256:(b + 1) * 256]) for b in range(nblk)], axis=0)
        col3 = col.reshape(rows_n, width, POOL_GROUP)
        acc = None
        for kk in range(w):
            s = kk - w // 2
            if abs(s) >= rows_n:
                continue
            if s == 0:
                term = col3
            elif s > 0:
                term = jnp.concatenate([col3[s:], jnp.zeros((s, width, POOL_GROUP), F32)], axis=0)
            else:
                term = jnp.concatenate([jnp.zeros((-s, width, POOL_GROUP), F32), col3[:s]], axis=0)
            acc = term if acc is None else acc + term
        mean = acc.reshape(seq, POOL_GROUP) * inv_ref[g]
        dlt = (mean - ug.astype(F32)).astype(BF16)
        o_ref[0, :, cols] = (_dot(dlt, pw_ref[g]) * ps_ref[:, cols]).astype(o_ref.dtype)


def _pool_consts(seq, width):
    rows_n = seq // width
    t = np.arange(256)
    tr, tc = t // width, t % width
    tt = np.arange(seq)
    row, colp = tt // width, tt % width
    mats, invs = [], []
    for w in POOL_WINDOWS:
        lo = np.clip(tc - w // 2, 0, width)
        hi = np.clip(tc - w // 2 + w, 0, width)
        m = (tr[:, None] == tr[None, :]) & (tc[None, :] >= lo[:, None]) & (tc[None, :] < hi[:, None])
        mats.append(m.astype(np.float32))
        cl, ch = np.clip(colp - w // 2, 0, width), np.clip(colp - w // 2 + w, 0, width)
        rl, rh = np.clip(row - w // 2, 0, rows_n), np.clip(row - w // 2 + w, 0, rows_n)
        cnt = ((rh - rl) * (ch - cl)).astype(np.float64)
        invs.append(np.broadcast_to((1.0 / cnt).astype(np.float32)[:, None], (seq, POOL_GROUP)))
    return jnp.asarray(np.stack(mats), dtype=BF16), jnp.asarray(np.stack(invs), dtype=F32)


def _pool(u, pool_w, pool_scale, width):
    bsz, seq, _ = u.shape
    amat, inv = _pool_consts(seq, width)
    ng = len(POOL_WINDOWS)
    return pl.pallas_call(
        functools.partial(_pool_kernel, seq=seq, width=width),
        grid=(bsz,),
        in_specs=[
            pl.BlockSpec((1, seq, D_POOL), lambda b: (b, 0, 0)),
            _resident((ng, 256, 256)),
            _resident((ng, seq, POOL_GROUP)),
            _resident((ng, POOL_GROUP, POOL_GROUP)),
            _resident((1, D_POOL)),
        ],
        out_specs=pl.BlockSpec((1, seq, D_POOL), lambda b: (b, 0, 0)),
        out_shape=jax.ShapeDtypeStruct((bsz, seq, D_POOL), BF16),
        compiler_params=_cparams(("parallel",)),
        name="pool",
    )(u, amat, inv, pool_w, pool_scale)


def _merge_kernel(yh_ref, yg_ref, yp_ref, sg_ref, x_ref, g1_ref, wh_ref, wg_ref, wp_ref, wo_ref, o_ref):
    m = sg_ref[:, 0:D_MODEL].astype(F32) * _dot(yh_ref[...], wh_ref[...])
    m = m + sg_ref[:, D_MODEL:2 * D_MODEL].astype(F32) * _dot(yg_ref[...], wg_ref[...])
    m = m + sg_ref[:, 2 * D_MODEL:3 * D_MODEL].astype(F32) * _dot(yp_ref[...], wp_ref[...])
    o_ref[...] = x_ref[...] + g1_ref[0] * _dot(m.astype(BF16), wo_ref[...])


def _merge(y_hy, y_gla, y_pool, sig, x2d, g1, w_hy, w_gla, w_pool, w_out, tokens_per_batch, tm):
    n_tok = x2d.shape[0]
    tpb = tokens_per_batch // tm

    def tok(width):
        return pl.BlockSpec((tm, width), lambda i: (i, 0))

    return pl.pallas_call(
        _merge_kernel,
        grid=(n_tok // tm,),
        in_specs=[tok(D_HY), tok(D_GLA_V), tok(D_POOL), tok(3 * D_MODEL), tok(D_MODEL),
                  pl.BlockSpec((1, 1, D_MODEL), lambda i: (i // tpb, 0, 0)),
                  _resident((D_HY, D_MODEL)), _resident((D_GLA_V, D_MODEL)), _resident((D_POOL, D_MODEL)),
                  _resident((D_MODEL, D_MODEL))],
        out_specs=tok(D_MODEL),
        out_shape=jax.ShapeDtypeStruct((n_tok, D_MODEL), F32),
        compiler_params=_cparams(("parallel",)),
        name="merge",
    )(y_hy, y_gla, y_pool, sig, x2d, g1, w_hy, w_gla, w_pool, w_out)


def _mlp_kernel(x_ref, sh_ref, sc_ref, g2_ref, gn_ref, wu_ref, wd_ref, nf_ref, o_ref, *, final_norm):
    x = x_ref[...]
    h = _ada_norm(x, gn_ref[...], sh_ref[0], sc_ref[0]).astype(BF16)
    step = 1024
    acc = jnp.zeros(x.shape, F32)
    for c0 in range(0, D_FF, step):
        u = jnp.maximum(_dot(h, wu_ref[:, c0:c0 + step]), 0.0)
        acc = acc + _dot((u * u).astype(BF16), wd_ref[c0:c0 + step, :])
    y = x + g2_ref[0] * acc
    if final_norm:
        y = y * lax.rsqrt(jnp.mean(y * y, axis=-1, keepdims=True) + RMS_EPS) * nf_ref[...]
    o_ref[...] = y


def _mlp(x2d, shift, scale, g2, gn, w_up, w_down, norm_final, tokens_per_batch, tm, final_norm):
    n_tok = x2d.shape[0]
    tpb = tokens_per_batch // tm

    def per_batch():
        return pl.BlockSpec((1, 1, D_MODEL), lambda i: (i // tpb, 0, 0))

    return pl.pallas_call(
        functools.partial(_mlp_kernel, final_norm=final_norm),
        grid=(n_tok // tm,),
        in_specs=[pl.BlockSpec((tm, D_MODEL), lambda i: (i, 0)), per_batch(), per_batch(), per_batch(),
                  _resident((1, D_MODEL)), _resident((D_MODEL, D_FF)), _resident((D_FF, D_MODEL)),
                  _resident((1, D_MODEL))],
        out_specs=pl.BlockSpec((tm, D_MODEL), lambda i: (i, 0)),
        out_shape=jax.ShapeDtypeStruct((n_tok, D_MODEL), F32),
        compiler_params=_cparams(("parallel",)),
        name="mlp",
    )(x2d, shift, scale, g2, gn, w_up, w_down, norm_final)


SEG_HY = (HY_COLS, BF16, False)
SEG_GLA = (GLA_MAIN, BF16, False)
SEG_RANK = (RANK_PAD, F32, False)
SEG_POOL = (D_POOL, BF16, False)
SEG_GATE = (3 * D_MODEL, BF16, True)


def _hy_block(seq):
    return 512 if seq % 512 == 0 and seq >= 2048 else 256


def _token_tile(seq):
    return 512 if seq % 512 == 0 else 256


def _mixers(pieces, lp, seq, width, s0):
    p_hy, p_gla, p_rank, p_pool = pieces
    n_tok = p_hy.shape[0]
    bsz = n_tok // seq
    blk = _hy_block(seq)
    fmat, gmat = _dft_mats(blk)
    y_hy = _hyena(p_hy.reshape(bsz, seq, HY_COLS), lp["hy_short_w"], lp["hy_short_b"], lp["hy_skip"],
                  fmat, gmat, lp["kspec"][seq], blk)
    y_gla, s_fin = _gla(p_gla.reshape(bsz, seq, GLA_MAIN), p_rank.reshape(bsz, seq, RANK_PAD),
                        lp["wa_pad"], lp["ba"], lp["gla_norm_w"], s0)
    y_pool = _pool(p_pool.reshape(bsz, seq, D_POOL), lp["pool_w"], lp["pool_scale"], width)
    return (y_hy.reshape(n_tok, D_HY), y_gla.reshape(n_tok, D_GLA_V), y_pool.reshape(n_tok, D_POOL)), s_fin


def kernel(x, c, ctx, c_ctx, w_mod, b_mod, norm_mix, norm_ffn, w_in, b_in, hy_short_w, hy_short_b, hy_f_w1, hy_f_b1, hy_f_freq, hy_f_w2, hy_f_b2, hy_f_w3, hy_skip, gla_wa_f, gla_ba_f, gla_wa_b, gla_ba_b, gla_norm_w, pool_w, pool_scale, w_br_hy, w_br_gla, w_br_pool, w_out, w_up, w_down, norm_final):
    bsz, seq, _ = x.shape
    ctx_len = ctx.shape[1]
    tm_x, tm_c = _token_tile(seq), _token_tile(ctx_len)

    cc = jnp.concatenate([c, c_ctx[None, :], jnp.zeros((MOD_ROWS - bsz - 1, D_MODEL), F32)], axis=0)
    mod = _modulation(cc, w_mod, b_mod)

    x2d = x.reshape(bsz * seq, D_MODEL)
    c2d = ctx.reshape(bsz * ctx_len, D_MODEL)
    s_zero = jnp.zeros((bsz, 2, GLA_HEADS, GLA_DV, 2 * GLA_DK), F32)
    nf = norm_final.reshape(1, D_MODEL)

    for l in range(DEPTH):
        mx = [mod[l, :bsz, i * D_MODEL:(i + 1) * D_MODEL].reshape(bsz, 1, D_MODEL) for i in range(6)]
        mc = [jnp.broadcast_to(mod[l, bsz:bsz + 1, i * D_MODEL:(i + 1) * D_MODEL].reshape(1, 1, D_MODEL),
                               (bsz, 1, D_MODEL)) for i in range(6)]
        wl, bl = w_in[l], b_in[l]
        w_rank = jnp.pad(wl[:, RANK_OFF:POOL_OFF], ((0, 0), (0, RANK_PAD - 2 * GLA_GATE_RANK)))
        b_rank = jnp.pad(bl[RANK_OFF:POOL_OFF], (0, RANK_PAD - 2 * GLA_GATE_RANK))
        seg_w = {"hy": wl[:, :GLA_OFF], "gla": wl[:, GLA_OFF:RANK_OFF], "rank": w_rank,
                 "pool": wl[:, POOL_OFF:GATE_OFF], "gate": wl[:, GATE_OFF:]}
        seg_b = {"hy": bl[:GLA_OFF], "gla": bl[GLA_OFF:RANK_OFF], "rank": b_rank,
                 "pool": bl[POOL_OFF:GATE_OFF], "gate": bl[GATE_OFF:]}
        seg_w = {k: v.astype(BF16) for k, v in seg_w.items()}
        seg_b = {k: v.reshape(1, -1) for k, v in seg_b.items()}

        wa_pad = jnp.zeros((2, RANK_PAD, D_GLA_K), F32)
        wa_pad = wa_pad.at[0, :GLA_GATE_RANK].set(gla_wa_f[l])
        wa_pad = wa_pad.at[1, GLA_GATE_RANK:2 * GLA_GATE_RANK].set(gla_wa_b[l])
        filt_args = (hy_f_w1[l], hy_f_b1[l], hy_f_freq[l], hy_f_w2[l], hy_f_b2[l], hy_f_w3[l])
        lp = {
            "hy_short_w": hy_short_w[l], "hy_short_b": hy_short_b[l].reshape(1, HY_COLS), "hy_skip": hy_skip[l],
            "wa_pad": wa_pad, "ba": jnp.stack([gla_ba_f[l], gla_ba_b[l]]).reshape(2, 1, D_GLA_K),
            "gla_norm_w": gla_norm_w[l].reshape(1, GLA_DV),
            "pool_w": pool_w[l].astype(BF16), "pool_scale": pool_scale[l].reshape(1, D_POOL),
            "kspec": {},
        }
        lp["kspec"][seq] = _filter_spectra(_hyena_filters(seq, *filt_args), seq, _hy_block(seq))
        gm = norm_mix[l].reshape(1, D_MODEL)
        gn = norm_ffn[l].reshape(1, D_MODEL)
        wbh, wbg, wbp = w_br_hy[l].astype(BF16), w_br_gla[l].astype(BF16), w_br_pool[l].astype(BF16)
        wo, wu, wd = w_out[l].astype(BF16), w_up[l].astype(BF16), w_down[l].astype(BF16)
        names = ("hy", "gla", "rank", "pool", "gate")
        segs = (SEG_HY, SEG_GLA, SEG_RANK, SEG_POOL, SEG_GATE)

        if l == DEPTH - 1:
            pc_gla, pc_rank = _in_proj(c2d, mc[0], mc[1], gm, [seg_w["gla"], seg_w["rank"]],
                                       [seg_b["gla"], seg_b["rank"]], (SEG_GLA, SEG_RANK), ctx_len, tm_c)
            _, s_ctx = _gla(pc_gla.reshape(bsz, ctx_len, GLA_MAIN), pc_rank.reshape(bsz, ctx_len, RANK_PAD),
                            lp["wa_pad"], lp["ba"], lp["gla_norm_w"], s_zero)
        else:
            lp["kspec"][ctx_len] = _filter_spectra(_hyena_filters(ctx_len, *filt_args), ctx_len,
                                                   _hy_block(ctx_len))
            pc = _in_proj(c2d, mc[0], mc[1], gm, [seg_w[n] for n in names], [seg_b[n] for n in names],
                          segs, ctx_len, tm_c)
            ys, s_ctx = _mixers(pc[:4], lp, ctx_len, ctx_len, s_zero)
            c2d = _merge(*ys, pc[4], c2d, mc[2], wbh, wbg, wbp, wo, ctx_len, tm_c)
            c2d = _mlp(c2d, mc[3], mc[4], mc[5], gn, wu, wd, nf, ctx_len, tm_c, False)

        px = _in_proj(x2d, mx[0], mx[1], gm, [seg_w[n] for n in names], [seg_b[n] for n in names],
                      segs, seq, tm_x)
        ys, _ = _mixers(px[:4], lp, seq, GRID_W, s_ctx)
        x2d = _merge(*ys, px[4], x2d, mx[2], wbh, wbg, wbp, wo, seq, tm_x)
        x2d = _mlp(x2d, mx[3], mx[4], mx[5], gn, wu, wd, nf, seq, tm_x, l == DEPTH - 1)
    return x2d.reshape(bsz, seq, D_MODEL)
```

```python
import functools
import math

import numpy as np
import jax
import jax.numpy as jnp
from jax import lax
from jax.experimental import pallas as pl
from jax.experimental.pallas import tpu as pltpu

F32 = jnp.float32
BF16 = jnp.bfloat16

D_MODEL = 1024
DEPTH = 2
GRID_W = 64
RMS_EPS = 1e-6

D_HY = 512
HY_BANDS = 8
HY_DECAY_TARGET = 1e-2
HY_FAST_DECAY = 0.3
HY_SLOW_DECAY = 1.5

GLA_HEADS = 4
GLA_DK = 64
GLA_DV = 128
D_GLA_K = GLA_HEADS * GLA_DK
D_GLA_V = GLA_HEADS * GLA_DV
GLA_GATE_RANK = 16
GLA_GATE_NORM = 16.0
GLA_CHUNK = 64
GLA_BLOCK = 256

POOL_WINDOWS = (2, 4, 8, 16)
POOL_GROUP = 128
D_POOL = 512
D_FF = 4 * D_MODEL

HY_COLS = 3 * D_HY
GLA_MAIN = 2 * D_GLA_K + 2 * D_GLA_V
GLA_OFF = HY_COLS
RANK_OFF = GLA_OFF + GLA_MAIN
POOL_OFF = RANK_OFF + 2 * GLA_GATE_RANK
GATE_OFF = POOL_OFF + D_POOL
N_IN = GATE_OFF + 3 * D_MODEL
RANK_PAD = 128

VMEM_LIMIT_V7X = 56 * 1024 * 1024
HY_CT = 256
FREQ_ROWS = 32
MOD_ROWS = 40
EMB_PAD = 128


def _cparams(sem):
    return pltpu.CompilerParams(dimension_semantics=sem, vmem_limit_bytes=VMEM_LIMIT_V7X)


def _dot(a, b):
    return jnp.dot(a, b, preferred_element_type=F32)


def _dot_nt(a, b):
    return lax.dot_general(a, b, (((1,), (1,)), ((), ())), preferred_element_type=F32)


def _dot_tn(a, b):
    return lax.dot_general(a, b, (((0,), (0,)), ((), ())), preferred_element_type=F32)


def _split(a):
    hi = a.astype(BF16)
    lo = (a - hi.astype(F32)).astype(BF16)
    return hi, lo


def _dot3(a, b):
    a_hi, a_lo = _split(a)
    b_hi, b_lo = _split(b)
    return _dot(a_hi, b_hi) + _dot(a_lo, b_hi) + _dot(a_hi, b_lo)


def _resident(shape):
    nd = len(shape)
    return pl.BlockSpec(shape, lambda *_: (0,) * nd, pipeline_mode=pl.Buffered(1))


def _mod_kernel(c_ref, w_ref, b_ref, o_ref):
    c = c_ref[...]
    o_ref[0] = _dot3(c * jax.nn.sigmoid(c), w_ref[0]) + b_ref[0]


def _modulation(cc, w_mod, b_mod):
    tn = 1536
    n = w_mod.shape[-1]
    return pl.pallas_call(
        _mod_kernel,
        grid=(DEPTH, n // tn),
        in_specs=[
            pl.BlockSpec((MOD_ROWS, D_MODEL), lambda l, j: (0, 0)),
            pl.BlockSpec((1, D_MODEL, tn), lambda l, j: (l, 0, j)),
            pl.BlockSpec((1, 1, tn), lambda l, j: (l, 0, j)),
        ],
        out_specs=pl.BlockSpec((1, MOD_ROWS, tn), lambda l, j: (l, 0, j)),
        out_shape=jax.ShapeDtypeStruct((DEPTH, MOD_ROWS, n), F32),
        compiler_params=_cparams(("arbitrary", "arbitrary")),
        name="modulation",
    )(cc, w_mod, b_mod.reshape(DEPTH, 1, n))


def _ada_norm(x, g, shift, scale):
    ms = jnp.mean(x * x, axis=-1, keepdims=True)
    return x * lax.rsqrt(ms + RMS_EPS) * g * (1.0 + scale) + shift


HALO = 8


def _in_kernel(x_ref, xp_ref, xn_ref, sh_ref, sc_ref, g_ref, cw_ref, cb_ref, *refs, segs, tpb):
    ns = len(segs)
    w_refs, b_refs, o_refs = refs[:ns], refs[ns:2 * ns], refs[2 * ns:]
    tm = x_ref.shape[0]
    g, sh, sc = g_ref[...], sh_ref[0], sc_ref[0]
    hm = _ada_norm(x_ref[...], g, sh, sc)
    h = hm.astype(BF16)
    for (width, _, mode), w_ref, b_ref, o_ref in zip(segs, w_refs, b_refs, o_refs):
        step = min(width, 768)
        if mode == "conv":
            h_ext = jnp.concatenate([_ada_norm(xp_ref[...], g, sh, sc), hm, _ada_norm(xn_ref[...], g, sh, sc)],
                                    axis=0).astype(BF16)
            pos = pl.program_id(0) % tpb
            row = lax.broadcasted_iota(jnp.int32, (tm + 2 * HALO, step), 0)
            no_prev = row == jnp.where(pos == 0, HALO, -1)
            no_next = row == jnp.where(pos == tpb - 1, HALO + tm - 1, -1)
        for c0 in range(0, width, step):
            cs = slice(c0, c0 + step)
            if mode == "conv":
                p = _dot(h_ext, w_ref[:, cs]) + b_ref[:, cs]
                prev = jnp.where(no_prev, 0.0, pltpu.roll(p, 1, 0))
                nxt = jnp.where(no_next, 0.0, pltpu.roll(p, tm + 2 * HALO - 1, 0))
                acc = cw_ref[0:1, cs] * prev + cw_ref[1:2, cs] * p + cw_ref[2:3, cs] * nxt + cb_ref[:, cs]
                acc = acc[HALO:HALO + tm]
            else:
                acc = _dot(h, w_ref[:, cs]) + b_ref[:, cs]
                if mode == "sigmoid":
                    acc = jax.nn.sigmoid(acc)
            o_ref[:, cs] = acc.astype(o_ref.dtype)


def _in_proj(x2d, shift, scale, g, conv_w, conv_b, weights, biases, segs, tokens_per_batch, tm):
    n_tok = x2d.shape[0]
    tpb = tokens_per_batch // tm
    hb = tm // HALO
    last_halo = n_tok // HALO - 1
    in_specs = [
        pl.BlockSpec((tm, D_MODEL), lambda i: (i, 0)),
        pl.BlockSpec((HALO, D_MODEL), lambda i: (jnp.maximum(i * hb - 1, 0), 0)),
        pl.BlockSpec((HALO, D_MODEL), lambda i: (jnp.minimum((i + 1) * hb, last_halo), 0)),
        pl.BlockSpec((1, 1, D_MODEL), lambda i: (i // tpb, 0, 0)),
        pl.BlockSpec((1, 1, D_MODEL), lambda i: (i // tpb, 0, 0)),
        _resident((1, D_MODEL)),
        _resident(conv_w.shape), _resident(conv_b.shape),
    ]
    in_specs += [_resident((D_MODEL, s[0])) for s in segs]
    in_specs += [_resident((1, s[0])) for s in segs]
    return pl.pallas_call(
        functools.partial(_in_kernel, segs=segs, tpb=tpb),
        grid=(n_tok // tm,),
        in_specs=in_specs,
        out_specs=[pl.BlockSpec((tm, s[0]), lambda i: (i, 0)) for s in segs],
        out_shape=[jax.ShapeDtypeStruct((n_tok, s[0]), s[1]) for s in segs],
        compiler_params=_cparams(("parallel",)),
        name="in_proj",
    )(x2d, x2d, x2d, shift, scale, g, conv_w, conv_b, *weights, *biases)


def _hy_kernel(x1_ref, x2_ref, v_ref, skip_ref, f_ref, g_ref, k_ref, o_ref, z_ref, uf_ref, yf_ref, *, seq, blk):
    nb = seq // blk

    def long_conv(read_u, gate_ref, order, write):
        for j in range(nb):
            uf_ref[j] = _dot(f_ref[...], read_u(j).astype(BF16))
        for i in range(nb):
            for r0 in range(0, blk, FREQ_ROWS):
                rows, rows_im = slice(r0, r0 + FREQ_ROWS), slice(blk + r0, blk + r0 + FREQ_ROWS)
                re = im = None
                for j in range(nb):
                    lag = i - j + nb - 1
                    kre, kim = k_ref[order, lag, 0, rows, :], k_ref[order, lag, 1, rows, :]
                    ure, uim = uf_ref[j, rows, :], uf_ref[j, rows_im, :]
                    if r0 == 0:
                        first = lax.broadcasted_iota(jnp.int32, (FREQ_ROWS, HY_CT), 0) == 0
                        nyq = uim * kim
                        t_re = ure * kre - jnp.where(first, 0.0, nyq)
                        t_im = jnp.where(first, nyq, ure * kim + uim * kre)
                    else:
                        t_re = ure * kre - uim * kim
                        t_im = ure * kim + uim * kre
                    re = t_re if re is None else re + t_re
                    im = t_im if im is None else im + t_im
                yf_ref[i, rows, :] = re.astype(BF16)
                yf_ref[i, rows_im, :] = im.astype(BF16)
            y = _dot(g_ref[...], yf_ref[i])
            u = read_u(i).astype(F32)
            gate = gate_ref[0, i * blk:(i + 1) * blk, :].astype(F32)
            write(i, gate * (y + u * skip_ref[order:order + 1, :]))

    def write_z(i, val):
        z_ref[i * blk:(i + 1) * blk, :] = val

    def write_o(i, val):
        o_ref[0, i * blk:(i + 1) * blk, :] = val.astype(o_ref.dtype)

    long_conv(lambda j: v_ref[0, j * blk:(j + 1) * blk, :], x1_ref, 0, write_z)
    long_conv(lambda j: z_ref[j * blk:(j + 1) * blk, :], x2_ref, 1, write_o)


def _hyena(p_hy, skip, fmat, gmat, kspec, blk):
    bsz, seq, _ = p_hy.shape
    nb = seq // blk
    nlag = 2 * nb - 1
    nct = D_HY // HY_CT

    def col(off):
        return pl.BlockSpec((1, seq, HY_CT), lambda c, b, off=off: (b, 0, off + c))

    in_specs = [col(0), col(nct), col(2 * nct),
                pl.BlockSpec((2, HY_CT), lambda c, b: (0, c)),
                _resident((2 * blk, blk)), _resident((blk, 2 * blk)),
                pl.BlockSpec((2, nlag, 2, blk, HY_CT), lambda c, b: (0, 0, 0, 0, c),
                             pipeline_mode=pl.Buffered(1))]
    return pl.pallas_call(
        functools.partial(_hy_kernel, seq=seq, blk=blk),
        grid=(nct, bsz),
        in_specs=in_specs,
        out_specs=pl.BlockSpec((1, seq, HY_CT), lambda c, b: (b, 0, c)),
        out_shape=jax.ShapeDtypeStruct((bsz, seq, D_HY), BF16),
        scratch_shapes=[pltpu.VMEM((seq, HY_CT), F32), pltpu.VMEM((nb, 2 * blk, HY_CT), F32),
                        pltpu.VMEM((nb, 2 * blk, HY_CT), BF16)],
        compiler_params=_cparams(("arbitrary", "arbitrary")),
        name="hyena",
    )(p_hy, p_hy, p_hy, skip, fmat, gmat, kspec)


def _dft_mats(blk):
    n = 2 * blk
    f = lax.broadcasted_iota(jnp.int32, (blk, blk), 0)
    t = lax.broadcasted_iota(jnp.int32, (blk, blk), 1)
    ang = ((f * t) % n).astype(F32) * (2.0 * math.pi / n)
    alt = jnp.where(t % 2 == 0, 1.0, -1.0).astype(F32)
    fre = jnp.cos(ang)
    fim = jnp.where(f == 0, alt, -jnp.sin(ang))
    fmat = jnp.concatenate([fre, fim], axis=0).astype(BF16)
    gre = jnp.where(f == 0, 1.0 / n, (2.0 / n) * jnp.cos(ang)).T
    gim = jnp.where(f == 0, alt / n, -(2.0 / n) * jnp.sin(ang)).T
    gmat = jnp.concatenate([gre, gim], axis=1).astype(BF16)
    return fmat, gmat


def _filter_kernel(z_ref, w1_ref, b1_ref, fr_ref, w2_ref, b2_ref, w3_ref, dl_ref, o_ref):
    z = z_ref[...]
    fr = fr_ref[0]
    a = jnp.sin(fr * (_dot3(z, w1_ref[0]) + b1_ref[0]))
    a = jnp.sin(fr * (_dot3(a, w2_ref[0]) + b2_ref[0]))
    h = _dot3(a, w3_ref[0])
    win = jnp.exp(-z[:, 0:1] * dl_ref[...])
    for q in range(4):
        o_ref[0, q] = h[:, q * D_HY:(q + 1) * D_HY] * win


def _hyena_filters(seq, w1, b1, freq, w2, b2, w3):
    t = jnp.arange(seq, dtype=F32)[:, None]
    bands = jnp.arange(1, HY_BANDS + 1, dtype=F32)[None, :]
    ang = (2.0 * math.pi / seq) * t * bands
    z = jnp.concatenate([t / seq, jnp.cos(ang), jnp.sin(ang),
                         jnp.zeros((seq, EMB_PAD - 1 - 2 * HY_BANDS), F32)], axis=-1)
    deltas = jnp.abs(jnp.linspace(math.log(HY_DECAY_TARGET) / HY_SLOW_DECAY,
                                  math.log(HY_DECAY_TARGET) / HY_FAST_DECAY, D_HY, dtype=F32)).reshape(1, D_HY)
    w1p = jnp.pad(w1, ((0, 0), (0, EMB_PAD - w1.shape[1]), (0, 0)))
    fh = w1.shape[-1]
    rt = min(seq, 512)

    def per_layer(shape):
        return pl.BlockSpec((1,) + shape, lambda l, r: (l,) + (0,) * len(shape))

    return pl.pallas_call(
        _filter_kernel,
        grid=(DEPTH, seq // rt),
        in_specs=[pl.BlockSpec((rt, EMB_PAD), lambda l, r: (r, 0)),
                  per_layer((EMB_PAD, fh)), per_layer((1, fh)), per_layer((1, fh)),
                  per_layer((fh, fh)), per_layer((1, fh)), per_layer((fh, 4 * D_HY)),
                  pl.BlockSpec((1, D_HY), lambda l, r: (0, 0))],
        out_specs=pl.BlockSpec((1, 4, rt, D_HY), lambda l, r: (l, 0, r, 0)),
        out_shape=jax.ShapeDtypeStruct((DEPTH, 4, seq, D_HY), F32),
        compiler_params=_cparams(("arbitrary", "arbitrary")),
        name="hyena_filters",
    )(z, w1p, b1.reshape(DEPTH, 1, fh), freq.reshape(DEPTH, 1, fh), w2, b2.reshape(DEPTH, 1, fh), w3, deltas)


def _spec_kernel(kl_ref, kr_ref, fl_ref, fr_ref, o_ref):
    def apply(f_ref, x):
        x_hi, x_lo = _split(x)
        return _dot(f_ref[0], x_hi) + _dot(f_ref[1], x_hi) + _dot(f_ref[0], x_lo)

    res = apply(fl_ref, kl_ref[0, 0]) + apply(fr_ref, kr_ref[0, 0])
    blk = res.shape[0] // 2
    o_ref[0, 0, 0, 0] = res[:blk]
    o_ref[0, 0, 0, 1] = res[blk:]


def _filter_spectra(h, seq, blk):
    nb = seq // blk
    nlag = 2 * nb - 1
    n = 2 * blk
    kk = jnp.concatenate([jnp.zeros((DEPTH, 2, 1, D_HY), F32), h[:, 2:4, :0:-1], h[:, 0:2]], axis=2)
    f = lax.broadcasted_iota(jnp.int32, (blk, n), 0)
    q = lax.broadcasted_iota(jnp.int32, (blk, n), 1)
    ang = ((f * q) % n).astype(F32) * (2.0 * math.pi / n)
    sgn = jnp.where(f % 2 == 0, 1.0, -1.0).astype(F32)
    alt = jnp.where(q % 2 == 0, 1.0, -1.0).astype(F32)
    fre = sgn * jnp.cos(ang)
    fim = jnp.where(f == 0, alt, -sgn * jnp.sin(ang))
    fre, fim = jnp.where(q == 0, 0.0, fre), jnp.where(q == 0, 0.0, fim)
    fk = jnp.concatenate([fre, fim], axis=0)
    fk_hi = fk.astype(BF16)
    fk_lo = (fk - fk_hi.astype(F32)).astype(BF16)
    fk2 = jnp.stack([fk_hi, fk_lo])
    return pl.pallas_call(
        _spec_kernel,
        grid=(DEPTH, 2, nlag),
        in_specs=[pl.BlockSpec((1, 1, blk, D_HY), lambda l, o, m: (l, o, m, 0)),
                  pl.BlockSpec((1, 1, blk, D_HY), lambda l, o, m: (l, o, m + 1, 0)),
                  _resident((2, n, blk)), _resident((2, n, blk))],
        out_specs=pl.BlockSpec((1, 1, 1, 2, blk, D_HY), lambda l, o, m: (l, o, m, 0, 0, 0)),
        out_shape=jax.ShapeDtypeStruct((DEPTH, 2, nlag, 2, blk, D_HY), F32),
        compiler_params=_cparams(("arbitrary", "arbitrary", "arbitrary")),
        name="filter_spectra",
    )(kk, kk, fk2[:, :, :blk], fk2[:, :, blk:])


def _alternate(*gens):
    live = list(gens)
    while live:
        for g in list(live):
            try:
                next(g)
            except StopIteration:
                live.remove(g)


def _gla_kernel(qkvr_ref, g_ref, wa_ref, ba_ref, nw_ref, s0_ref, tri_ref, y_ref, s_ref, o_ref, prep_ref, dec_ref,
                *, seq):
    nblk = seq // GLA_BLOCK
    nchunk = GLA_BLOCK // GLA_CHUNK
    lane = lax.broadcasted_iota(jnp.int32, (1, 2 * GLA_DK), 1)
    head_mask = [(lane < GLA_DK).astype(BF16), (lane >= GLA_DK).astype(BF16)]
    ri = lax.broadcasted_iota(jnp.int32, (GLA_BLOCK, GLA_BLOCK), 0)
    ci = lax.broadcasted_iota(jnp.int32, (GLA_BLOCK, GLA_BLOCK), 1)
    s_ref[...] = s0_ref[...]
    Q_LOC, Q_INT, K_ST, K_STRIP = 0, 1, 2, 3

    def block_rows(bi):
        return pl.ds(pl.multiple_of(bi * GLA_BLOCK, GLA_BLOCK), GLA_BLOCK)

    def prep(d, bi, buf):
        rows = block_rows(bi)
        logit = _dot3(g_ref[0, rows, :], wa_ref[d]) + ba_ref[d]
        yield
        la = (jnp.minimum(logit, 0.0) - jnp.log(1.0 + jnp.exp(-jnp.abs(logit)))) * (1.0 / GLA_GATE_NORM)
        la_hi, la_lo = _split(la)
        yield
        cum_blk = _dot(tri_ref[d, 0], la_hi) + _dot(tri_ref[d, 0], la_lo)
        cum_loc = _dot(tri_ref[d, 1], la_hi) + _dot(tri_ref[d, 1], la_lo)
        yield
        q_loc = qkvr_ref[0, rows, 0:D_GLA_K].astype(F32) * (GLA_DK ** -0.5) * jnp.exp(cum_loc)
        prep_ref[buf, d, Q_LOC] = q_loc.astype(BF16)
        yield
        k_inv = qkvr_ref[0, rows, D_GLA_K:2 * D_GLA_K].astype(F32) * jnp.exp(-cum_loc)
        yield
        edge = GLA_CHUNK - 1 if d == 0 else 0
        firsts = [slice(c * GLA_CHUNK, c * GLA_CHUNK + 1) for c in range(nchunk)]
        bef = [cum_blk[r] - cum_loc[r] for r in firsts]
        tot = [cum_loc[c * GLA_CHUNK + edge:c * GLA_CHUNK + edge + 1] for c in range(nchunk)]
        last = nchunk - 1 if d == 0 else 0
        total = bef[last] + tot[last]
        dec_ref[buf, d, 0:1, :] = jnp.exp(total)
        for c in range(nchunk):
            rc = slice(c * GLA_CHUNK, (c + 1) * GLA_CHUNK)
            prep_ref[buf, d, Q_INT, rc, :] = (q_loc[rc] * jnp.exp(bef[c])).astype(BF16)
            k_end = k_inv[rc] * jnp.exp(tot[c])
            prep_ref[buf, d, K_ST, rc, :] = (k_end * jnp.exp(total - bef[c] - tot[c])).astype(BF16)
            for a in range(nchunk):
                earlier = c < a if d == 0 else c > a
                piece = k_end * jnp.exp(bef[a] - bef[c] - tot[c]) if earlier else k_inv[rc]
                prep_ref[buf, d, K_STRIP + a, rc, :] = piece.astype(BF16)
            yield

    def heads(d, bi, buf):
        rows = block_rows(bi)
        causal = (ci <= ri) if d == 0 else (ci >= ri)
        for h in range(GLA_HEADS):
            pair = slice((h // 2) * 2 * GLA_DK, (h // 2 + 1) * 2 * GLA_DK)
            hm = head_mask[h % 2]
            v_h = qkvr_ref[0, rows, 2 * D_GLA_K + h * GLA_DV:2 * D_GLA_K + (h + 1) * GLA_DV]
            strips = []
            for a in range(nchunk):
                rc = slice(a * GLA_CHUNK, (a + 1) * GLA_CHUNK)
                strips.append(_dot_nt(prep_ref[buf, d, Q_LOC, rc, pair] * hm, prep_ref[buf, d, K_STRIP + a, :, pair]))
            yield
            scores = jnp.where(causal, jnp.concatenate(strips, axis=0), 0.0).astype(BF16)
            st = s_ref[0, d, h]
            o_ref[d, rows, h * GLA_DV:(h + 1) * GLA_DV] = (
                _dot(scores, v_h) + _dot_nt(prep_ref[buf, d, Q_INT, :, pair] * hm, st.astype(BF16)))
            yield
            s_ref[0, d, h] = (dec_ref[buf, d, 0:1, pair] * st
                              + _dot_tn(v_h, prep_ref[buf, d, K_ST, :, pair] * hm))
            yield

    def step(it, buf):
        _alternate(heads(0, it, buf), prep(1, jnp.maximum(nblk - 2 - it, 0), 1 - buf),
                   heads(1, nblk - 1 - it, buf), prep(0, jnp.minimum(it + 1, nblk - 1), 1 - buf))

    _alternate(prep(0, 0, 0), prep(1, nblk - 1, 0))
    if nblk == 1:
        _alternate(heads(0, 0, 0), heads(1, 0, 0))
    else:
        def two_steps(i2, carry):
            step(2 * i2, 0)
            step(2 * i2 + 1, 1)
            return carry
        lax.fori_loop(0, nblk // 2, two_steps, 0)

    for h in range(GLA_HEADS):
        cols = slice(h * GLA_DV, (h + 1) * GLA_DV)
        o = o_ref[0, :, cols] + o_ref[1, :, cols]
        o = o * lax.rsqrt(jnp.mean(o * o, axis=-1, keepdims=True) + RMS_EPS) * nw_ref[...]
        r = qkvr_ref[0, :, 2 * D_GLA_K + D_GLA_V + h * GLA_DV:2 * D_GLA_K + D_GLA_V + (h + 1) * GLA_DV].astype(F32)
        y_ref[0, :, cols] = (o * (r * jax.nn.sigmoid(r))).astype(y_ref.dtype)


def _gla_consts():
    i = np.arange(GLA_BLOCK)
    same = (i[:, None] // GLA_CHUNK) == (i[None, :] // GLA_CHUNK)
    lower = i[None, :] <= i[:, None]
    tri = np.stack([np.stack([lower, lower & same]), np.stack([lower.T, lower.T & same])])
    return jnp.asarray(tri.astype(np.float32), dtype=BF16)


def _gla(qkvr, rank, wa_pad, ba, norm_w, s0):
    bsz, seq, _ = qkvr.shape
    st_shape = (bsz, 2, GLA_HEADS, GLA_DV, 2 * GLA_DK)
    return pl.pallas_call(
        functools.partial(_gla_kernel, seq=seq),
        grid=(bsz,),
        in_specs=[
            pl.BlockSpec((1, seq, GLA_MAIN), lambda b: (b, 0, 0)),
            pl.BlockSpec((1, seq, RANK_PAD), lambda b: (b, 0, 0)),
            _resident((2, RANK_PAD, D_GLA_K)),
            _resident((2, 1, D_GLA_K)),
            _resident((1, GLA_DV)),
            pl.BlockSpec((1,) + st_shape[1:], lambda b: (b, 0, 0, 0, 0)),
            _resident((2, 2, GLA_BLOCK, GLA_BLOCK)),
        ],
        out_specs=[pl.BlockSpec((1, seq, D_GLA_V), lambda b: (b, 0, 0)),
                   pl.BlockSpec((1,) + st_shape[1:], lambda b: (b, 0, 0, 0, 0))],
        out_shape=[jax.ShapeDtypeStruct((bsz, seq, D_GLA_V), BF16), jax.ShapeDtypeStruct(st_shape, F32)],
        scratch_shapes=[pltpu.VMEM((2, seq, D_GLA_V), F32),
                        pltpu.VMEM((2, 2, 3 + GLA_BLOCK // GLA_CHUNK, GLA_BLOCK, D_GLA_K), BF16),
                        pltpu.VMEM((2, 2, 8, D_GLA_K), F32)],
        compiler_params=_cparams(("parallel",)),
        name="gla",
    )(qkvr, rank, wa_pad, ba, norm_w, s0, _gla_consts())


def _pool_kernel(u_ref, a_ref, inv_ref, pw_ref, ps_ref, o_ref, *, seq, width):
    rows_n = seq // width
    nblk = seq // 256
    for g, w in enumerate(POOL_WINDOWS):
        cols = slice(g * POOL_GROUP, (g + 1) * POOL_GROUP)
        ug = u_ref[0, :, cols]
        col = jnp.concatenate([_dot(a_ref[g], ug[b * 256:(b + 1) * 256]) for b in range(nblk)], axis=0)
        col3 = col.reshape(rows_n, width, POOL_GROUP)
        acc = None
        for kk in range(w):
            s = kk - w // 2
            if abs(s) >= rows_n:
                continue
            if s == 0:
                term = col3
            elif s > 0:
                term = jnp.concatenate([col3[s:], jnp.zeros((s, width, POOL_GROUP), F32)], axis=0)
            else:
                term = jnp.concatenate([jnp.zeros((-s, width, POOL_GROUP), F32), col3[:s]], axis=0)
            acc = term if acc is None else acc + term
        mean = acc.reshape(seq, POOL_GROUP) * inv_ref[g]
        dlt = (mean - ug.astype(F32)).astype(BF16)
        o_ref[0, :, cols] = (_dot(dlt, pw_ref[g]) * ps_ref[:, cols]).astype(o_ref.dtype)


def _pool_consts(seq, width):
    rows_n = seq // width
    t = np.arange(256)
    tr, tc = t // width, t % width
    tt = np.arange(seq)
    row, colp = tt // width, tt % width
    mats, invs = [], []
    for w in POOL_WINDOWS:
        lo = np.clip(tc - w // 2, 0, width)
        hi = np.clip(tc - w // 2 + w, 0, width)
        m = (tr[:, None] == tr[None, :]) & (tc[None, :] >= lo[:, None]) & (tc[None, :] < hi[:, None])
        mats.append(m.astype(np.float32))
        cl, ch = np.clip(colp - w // 2, 0, width), np.clip(colp - w // 2 + w, 0, width)
        rl, rh = np.clip(row - w // 2, 0, rows_n), np.clip(row - w // 2 + w, 0, rows_n)
        cnt = ((rh - rl) * (ch - cl)).astype(np.float64)
        invs.append(np.broadcast_to((1.0 / cnt).astype(np.float32)[:, None], (seq, POOL_GROUP)))
    return jnp.asarray(np.stack(mats), dtype=BF16), jnp.asarray(np.stack(invs), dtype=F32)


def _pool(u, pool_w, pool_scale, width):
    bsz, seq, _ = u.shape
    amat, inv = _pool_consts(seq, width)
    ng = len(POOL_WINDOWS)
    return pl.pallas_call(
        functools.partial(_pool_kernel, seq=seq, width=width),
        grid=(bsz,),
        in_specs=[
            pl.BlockSpec((1, seq, D_POOL), lambda b: (b, 0, 0)),
            _resident((ng, 256, 256)),
            _resident((ng, seq, POOL_GROUP)),
            _resident((ng, POOL_GROUP, POOL_GROUP)),
            _resident((1, D_POOL)),
        ],
        out_specs=pl.BlockSpec((1, seq, D_POOL), lambda b: (b, 0, 0)),
        out_shape=jax.ShapeDtypeStruct((bsz, seq, D_POOL), BF16),
        compiler_params=_cparams(("parallel",)),
        name="pool",
    )(u, amat, inv, pool_w, pool_scale)


def _merge_kernel(yh_ref, yg_ref, yp_ref, sg_ref, x_ref, g1_ref, wh_ref, wg_ref, wp_ref, wo_ref, o_ref):
    m = sg_ref[:, 0:D_MODEL].astype(F32) * _dot(yh_ref[...], wh_ref[...])
    m = m + sg_ref[:, D_MODEL:2 * D_MODEL].astype(F32) * _dot(yg_ref[...], wg_ref[...])
    m = m + sg_ref[:, 2 * D_MODEL:3 * D_MODEL].astype(F32) * _dot(yp_ref[...], wp_ref[...])
    o_ref[...] = x_ref[...] + g1_ref[0] * _dot(m.astype(BF16), wo_ref[...])


def _merge(y_hy, y_gla, y_pool, sig, x2d, g1, w_hy, w_gla, w_pool, w_out, tokens_per_batch, tm):
    n_tok = x2d.shape[0]
    tpb = tokens_per_batch // tm

    def tok(width):
        return pl.BlockSpec((tm, width), lambda i: (i, 0))

    return pl.pallas_call(
        _merge_kernel,
        grid=(n_tok // tm,),
        in_specs=[tok(D_HY), tok(D_GLA_V), tok(D_POOL), tok(3 * D_MODEL), tok(D_MODEL),
                  pl.BlockSpec((1, 1, D_MODEL), lambda i: (i // tpb, 0, 0)),
                  _resident((D_HY, D_MODEL)), _resident((D_GLA_V, D_MODEL)), _resident((D_POOL, D_MODEL)),
                  _resident((D_MODEL, D_MODEL))],
        out_specs=tok(D_MODEL),
        out_shape=jax.ShapeDtypeStruct((n_tok, D_MODEL), F32),
        compiler_params=_cparams(("parallel",)),
        name="merge",
    )(y_hy, y_gla, y_pool, sig, x2d, g1, w_hy, w_gla, w_pool, w_out)


def _mlp_kernel(x_ref, sh_ref, sc_ref, g2_ref, gn_ref, wu_ref, wd_ref, nf_ref, o_ref, *, final_norm):
    x = x_ref[...]
    h = _ada_norm(x, gn_ref[...], sh_ref[0], sc_ref[0]).astype(BF16)
    step = 1024
    acc = jnp.zeros(x.shape, F32)
    for c0 in range(0, D_FF, step):
        u = jnp.maximum(_dot(h, wu_ref[:, c0:c0 + step]), 0.0)
        acc = acc + _dot((u * u).astype(BF16), wd_ref[c0:c0 + step, :])
    y = x + g2_ref[0] * acc
    if final_norm:
        y = y * lax.rsqrt(jnp.mean(y * y, axis=-1, keepdims=True) + RMS_EPS) * nf_ref[...]
    o_ref[...] = y


def _mlp(x2d, shift, scale, g2, gn, w_up, w_down, norm_final, tokens_per_batch, tm, final_norm):
    n_tok = x2d.shape[0]
    tpb = tokens_per_batch // tm

    def per_batch():
        return pl.BlockSpec((1, 1, D_MODEL), lambda i: (i // tpb, 0, 0))

    return pl.pallas_call(
        functools.partial(_mlp_kernel, final_norm=final_norm),
        grid=(n_tok // tm,),
        in_specs=[pl.BlockSpec((tm, D_MODEL), lambda i: (i, 0)), per_batch(), per_batch(), per_batch(),
                  _resident((1, D_MODEL)), _resident((D_MODEL, D_FF)), _resident((D_FF, D_MODEL)),
                  _resident((1, D_MODEL))],
        out_specs=pl.BlockSpec((tm, D_MODEL), lambda i: (i, 0)),
        out_shape=jax.ShapeDtypeStruct((n_tok, D_MODEL), F32),
        compiler_params=_cparams(("parallel",)),
        name="mlp",
    )(x2d, shift, scale, g2, gn, w_up, w_down, norm_final)


SEG_HY = (HY_COLS, BF16, "conv")
SEG_GLA = (GLA_MAIN, BF16, "plain")
SEG_RANK = (RANK_PAD, F32, "plain")
SEG_POOL = (D_POOL, BF16, "plain")
SEG_GATE = (3 * D_MODEL, BF16, "sigmoid")


def _hy_block(seq):
    return 512 if seq % 512 == 0 and seq >= 2048 else 256


def _token_tile(seq):
    return 512 if seq % 512 == 0 else 256


def _mixers(pieces, lp, seq, width, s0):
    p_hy, p_gla, p_rank, p_pool = pieces
    n_tok = p_hy.shape[0]
    bsz = n_tok // seq
    blk = _hy_block(seq)
    fmat, gmat = _dft_mats(blk)
    y_hy = _hyena(p_hy.reshape(bsz, seq, HY_COLS), lp["hy_skip"], fmat, gmat, lp["kspec"][seq], blk)
    y_gla, s_fin = _gla(p_gla.reshape(bsz, seq, GLA_MAIN), p_rank.reshape(bsz, seq, RANK_PAD),
                        lp["wa_pad"], lp["ba"], lp["gla_norm_w"], s0)
    y_pool = _pool(p_pool.reshape(bsz, seq, D_POOL), lp["pool_w"], lp["pool_scale"], width)
    return (y_hy.reshape(n_tok, D_HY), y_gla.reshape(n_tok, D_GLA_V), y_pool.reshape(n_tok, D_POOL)), s_fin


def kernel(x, c, ctx, c_ctx, w_mod, b_mod, norm_mix, norm_ffn, w_in, b_in, hy_short_w, hy_short_b, hy_f_w1, hy_f_b1, hy_f_freq, hy_f_w2, hy_f_b2, hy_f_w3, hy_skip, gla_wa_f, gla_ba_f, gla_wa_b, gla_ba_b, gla_norm_w, pool_w, pool_scale, w_br_hy, w_br_gla, w_br_pool, w_out, w_up, w_down, norm_final):
    bsz, seq, _ = x.shape
    ctx_len = ctx.shape[1]
    tm_x, tm_c = _token_tile(seq), _token_tile(ctx_len)

    cc = jnp.concatenate([c, c_ctx[None, :], jnp.zeros((MOD_ROWS - bsz - 1, D_MODEL), F32)], axis=0)
    mod = _modulation(cc, w_mod, b_mod)

    x2d = x.reshape(bsz * seq, D_MODEL)
    c2d = ctx.reshape(bsz * ctx_len, D_MODEL)
    s_zero = jnp.zeros((bsz, 2, GLA_HEADS, GLA_DV, 2 * GLA_DK), F32)
    nf = norm_final.reshape(1, D_MODEL)
    filt_args = (hy_f_w1, hy_f_b1, hy_f_freq, hy_f_w2, hy_f_b2, hy_f_w3)
    kspec_x = _filter_spectra(_hyena_filters(seq, *filt_args), seq, _hy_block(seq))
    kspec_c = _filter_spectra(_hyena_filters(ctx_len, *filt_args), ctx_len, _hy_block(ctx_len))

    for l in range(DEPTH):
        mx = [mod[l, :bsz, i * D_MODEL:(i + 1) * D_MODEL].reshape(bsz, 1, D_MODEL) for i in range(6)]
        mc = [jnp.broadcast_to(mod[l, bsz:bsz + 1, i * D_MODEL:(i + 1) * D_MODEL].reshape(1, 1, D_MODEL),
                               (bsz, 1, D_MODEL)) for i in range(6)]
        wl, bl = w_in[l], b_in[l]
        w_rank = jnp.pad(wl[:, RANK_OFF:POOL_OFF], ((0, 0), (0, RANK_PAD - 2 * GLA_GATE_RANK)))
        b_rank = jnp.pad(bl[RANK_OFF:POOL_OFF], (0, RANK_PAD - 2 * GLA_GATE_RANK))
        seg_w = {"hy": wl[:, :GLA_OFF], "gla": wl[:, GLA_OFF:RANK_OFF], "rank": w_rank,
                 "pool": wl[:, POOL_OFF:GATE_OFF], "gate": wl[:, GATE_OFF:]}
        seg_b = {"hy": bl[:GLA_OFF], "gla": bl[GLA_OFF:RANK_OFF], "rank": b_rank,
                 "pool": bl[POOL_OFF:GATE_OFF], "gate": bl[GATE_OFF:]}
        seg_w = {k: v.astype(BF16) for k, v in seg_w.items()}
        seg_b = {k: v.reshape(1, -1) for k, v in seg_b.items()}

        wa_pad = jnp.zeros((2, RANK_PAD, D_GLA_K), F32)
        wa_pad = wa_pad.at[0, :GLA_GATE_RANK].set(gla_wa_f[l])
        wa_pad = wa_pad.at[1, GLA_GATE_RANK:2 * GLA_GATE_RANK].set(gla_wa_b[l])
        lp = {
            "hy_skip": hy_skip[l],
            "wa_pad": wa_pad, "ba": jnp.stack([gla_ba_f[l], gla_ba_b[l]]).reshape(2, 1, D_GLA_K),
            "gla_norm_w": gla_norm_w[l].reshape(1, GLA_DV),
            "pool_w": pool_w[l].astype(BF16), "pool_scale": pool_scale[l].reshape(1, D_POOL),
            "kspec": {seq: kspec_x[l], ctx_len: kspec_c[l]},
        }
        cw, cb = hy_short_w[l], hy_short_b[l].reshape(1, HY_COLS)
        gm = norm_mix[l].reshape(1, D_MODEL)
        gn = norm_ffn[l].reshape(1, D_MODEL)
        wbh, wbg, wbp = w_br_hy[l].astype(BF16), w_br_gla[l].astype(BF16), w_br_pool[l].astype(BF16)
        wo, wu, wd = w_out[l].astype(BF16), w_up[l].astype(BF16), w_down[l].astype(BF16)
        names = ("hy", "gla", "rank", "pool", "gate")
        segs = (SEG_HY, SEG_GLA, SEG_RANK, SEG_POOL, SEG_GATE)

        if l == DEPTH - 1:
            pc_gla, pc_rank = _in_proj(c2d, mc[0], mc[1], gm, cw, cb, [seg_w["gla"], seg_w["rank"]],
                                       [seg_b["gla"], seg_b["rank"]], (SEG_GLA, SEG_RANK), ctx_len, tm_c)
            _, s_ctx = _gla(pc_gla.reshape(bsz, ctx_len, GLA_MAIN), pc_rank.reshape(bsz, ctx_len, RANK_PAD),
                            lp["wa_pad"], lp["ba"], lp["gla_norm_w"], s_zero)
        else:
            pc = _in_proj(c2d, mc[0], mc[1], gm, cw, cb, [seg_w[n] for n in names], [seg_b[n] for n in names],
                          segs, ctx_len, tm_c)
            ys, s_ctx = _mixers(pc[:4], lp, ctx_len, ctx_len, s_zero)
            c2d = _merge(*ys, pc[4], c2d, mc[2], wbh, wbg, wbp, wo, ctx_len, tm_c)
            c2d = _mlp(c2d, mc[3], mc[4], mc[5], gn, wu, wd, nf, ctx_len, tm_c, False)

        px = _in_proj(x2d, mx[0], mx[1], gm, cw, cb, [seg_w[n] for n in names], [seg_b[n] for n in names],
                      segs, seq, tm_x)
        ys, _ = _mixers(px[:4], lp, seq, GRID_W, s_ctx)
        x2d = _merge(*ys, px[4], x2d, mx[2], wbh, wbg, wbp, wo, seq, tm_x)
        x2d = _mlp(x2d, mx[3], mx[4], mx[5], gn, wu, wd, nf, seq, tm_x, l == DEPTH - 1)
    return x2d.reshape(bsz, seq, D_MODEL)
```

```python
import functools
import math

import numpy as np
import jax
import jax.numpy as jnp
from jax import lax
from jax.experimental import pallas as pl
from jax.experimental.pallas import tpu as pltpu

F32 = jnp.float32
BF16 = jnp.bfloat16

D_MODEL = 1024
DEPTH = 2
GRID_W = 64
RMS_EPS = 1e-6

D_HY = 512
HY_BANDS = 8
HY_DECAY_TARGET = 1e-2
HY_FAST_DECAY = 0.3
HY_SLOW_DECAY = 1.5

GLA_HEADS = 4
GLA_DK = 64
GLA_DV = 128
D_GLA_K = GLA_HEADS * GLA_DK
D_GLA_V = GLA_HEADS * GLA_DV
GLA_GATE_RANK = 16
GLA_GATE_NORM = 16.0
GLA_CHUNK = 64
GLA_BLOCK = 256

POOL_WINDOWS = (2, 4, 8, 16)
POOL_GROUP = 128
D_POOL = 512
D_FF = 4 * D_MODEL

HY_COLS = 3 * D_HY
GLA_MAIN = 2 * D_GLA_K + 2 * D_GLA_V
GLA_OFF = HY_COLS
RANK_OFF = GLA_OFF + GLA_MAIN
POOL_OFF = RANK_OFF + 2 * GLA_GATE_RANK
GATE_OFF = POOL_OFF + D_POOL
N_IN = GATE_OFF + 3 * D_MODEL
RANK_PAD = 128

VMEM_LIMIT_V7X = 56 * 1024 * 1024
HY_CT = 256
FREQ_ROWS = 32
MOD_ROWS = 40
EMB_PAD = 128


def _cparams(sem):
    return pltpu.CompilerParams(dimension_semantics=sem, vmem_limit_bytes=VMEM_LIMIT_V7X)


def _dot(a, b):
    return jnp.dot(a, b, preferred_element_type=F32)


def _dot_nt(a, b):
    return lax.dot_general(a, b, (((1,), (1,)), ((), ())), preferred_element_type=F32)


def _dot_tn(a, b):
    return lax.dot_general(a, b, (((0,), (0,)), ((), ())), preferred_element_type=F32)


def _split(a):
    hi = a.astype(BF16)
    lo = (a - hi.astype(F32)).astype(BF16)
    return hi, lo


def _dot3(a, b):
    a_hi, a_lo = _split(a)
    b_hi, b_lo = _split(b)
    return _dot(a_hi, b_hi) + _dot(a_lo, b_hi) + _dot(a_hi, b_lo)


def _resident(shape):
    nd = len(shape)
    return pl.BlockSpec(shape, lambda *_: (0,) * nd, pipeline_mode=pl.Buffered(1))


def _mod_kernel(c_ref, w_ref, b_ref, o_ref):
    c = c_ref[...]
    o_ref[0] = _dot3(c * jax.nn.sigmoid(c), w_ref[0]) + b_ref[0]


def _modulation(cc, w_mod, b_mod):
    tn = 1536
    n = w_mod.shape[-1]
    return pl.pallas_call(
        _mod_kernel,
        grid=(DEPTH, n // tn),
        in_specs=[
            pl.BlockSpec((MOD_ROWS, D_MODEL), lambda l, j: (0, 0)),
            pl.BlockSpec((1, D_MODEL, tn), lambda l, j: (l, 0, j)),
            pl.BlockSpec((1, 1, tn), lambda l, j: (l, 0, j)),
        ],
        out_specs=pl.BlockSpec((1, MOD_ROWS, tn), lambda l, j: (l, 0, j)),
        out_shape=jax.ShapeDtypeStruct((DEPTH, MOD_ROWS, n), F32),
        compiler_params=_cparams(("arbitrary", "arbitrary")),
        name="modulation",
    )(cc, w_mod, b_mod.reshape(DEPTH, 1, n))


def _ada_norm(x, g, shift, scale):
    ms = jnp.mean(x * x, axis=-1, keepdims=True)
    return x * lax.rsqrt(ms + RMS_EPS) * g * (1.0 + scale) + shift


HALO = 8


def _in_kernel(x_ref, xp_ref, xn_ref, sh_ref, sc_ref, g_ref, cw_ref, cb_ref, *refs, segs, tpb):
    ns = len(segs)
    w_refs, b_refs, o_refs = refs[:ns], refs[ns:2 * ns], refs[2 * ns:]
    tm = x_ref.shape[0]
    g, sh, sc = g_ref[...], sh_ref[0], sc_ref[0]
    hm = _ada_norm(x_ref[...], g, sh, sc)
    h = hm.astype(BF16)
    for (width, _, mode), w_ref, b_ref, o_ref in zip(segs, w_refs, b_refs, o_refs):
        step = min(width, 768)
        if mode == "conv":
            h_ext = jnp.concatenate([_ada_norm(xp_ref[...], g, sh, sc), hm, _ada_norm(xn_ref[...], g, sh, sc)],
                                    axis=0).astype(BF16)
            pos = pl.program_id(0) % tpb
            row = lax.broadcasted_iota(jnp.int32, (tm + 2 * HALO, step), 0)
            no_prev = row == jnp.where(pos == 0, HALO, -1)
            no_next = row == jnp.where(pos == tpb - 1, HALO + tm - 1, -1)
        for c0 in range(0, width, step):
            cs = slice(c0, c0 + step)
            if mode == "conv":
                p = _dot(h_ext, w_ref[:, cs]) + b_ref[:, cs]
                prev = jnp.where(no_prev, 0.0, pltpu.roll(p, 1, 0))
                nxt = jnp.where(no_next, 0.0, pltpu.roll(p, tm + 2 * HALO - 1, 0))
                acc = cw_ref[0:1, cs] * prev + cw_ref[1:2, cs] * p + cw_ref[2:3, cs] * nxt + cb_ref[:, cs]
                acc = acc[HALO:HALO + tm]
            else:
                acc = _dot(h, w_ref[:, cs]) + b_ref[:, cs]
            o_ref[:, cs] = acc.astype(o_ref.dtype)


def _in_proj(x2d, shift, scale, g, conv_w, conv_b, weights, biases, segs, tokens_per_batch, tm):
    n_tok = x2d.shape[0]
    tpb = tokens_per_batch // tm
    hb = tm // HALO
    last_halo = n_tok // HALO - 1
    in_specs = [
        pl.BlockSpec((tm, D_MODEL), lambda i: (i, 0)),
        pl.BlockSpec((HALO, D_MODEL), lambda i: (jnp.maximum(i * hb - 1, 0), 0)),
        pl.BlockSpec((HALO, D_MODEL), lambda i: (jnp.minimum((i + 1) * hb, last_halo), 0)),
        pl.BlockSpec((1, 1, D_MODEL), lambda i: (i // tpb, 0, 0)),
        pl.BlockSpec((1, 1, D_MODEL), lambda i: (i // tpb, 0, 0)),
        _resident((1, D_MODEL)),
        _resident(conv_w.shape), _resident(conv_b.shape),
    ]
    in_specs += [_resident((D_MODEL, s[0])) for s in segs]
    in_specs += [_resident((1, s[0])) for s in segs]
    return pl.pallas_call(
        functools.partial(_in_kernel, segs=segs, tpb=tpb),
        grid=(n_tok // tm,),
        in_specs=in_specs,
        out_specs=[pl.BlockSpec((tm, s[0]), lambda i: (i, 0)) for s in segs],
        out_shape=[jax.ShapeDtypeStruct((n_tok, s[0]), s[1]) for s in segs],
        compiler_params=_cparams(("parallel",)),
        name="in_proj",
    )(x2d, x2d, x2d, shift, scale, g, conv_w, conv_b, *weights, *biases)


def _hy_kernel(x1_ref, x2_ref, v_ref, skip_ref, f_ref, g_ref, k_ref, o_ref, z_ref, uf_ref, yf_ref, *, seq, blk):
    nb = seq // blk

    def long_conv(read_u, gate_ref, order, write):
        for j in range(nb):
            uf_ref[j] = _dot(f_ref[...], read_u(j).astype(BF16)).astype(BF16)
        for i in range(nb):
            for r0 in range(0, blk, FREQ_ROWS):
                rows, rows_im = slice(r0, r0 + FREQ_ROWS), slice(blk + r0, blk + r0 + FREQ_ROWS)
                re = im = None
                for j in range(nb):
                    lag = i - j + nb - 1
                    kre, kim, krx = (k_ref[order, lag, s, rows, :] for s in range(3))
                    ure, uim = uf_ref[j, rows, :], uf_ref[j, rows_im, :]
                    t_re = ure * kre - uim * kim
                    t_im = ure * kim + uim * krx
                    re = t_re if re is None else re + t_re
                    im = t_im if im is None else im + t_im
                yf_ref[i, rows, :] = re
                yf_ref[i, rows_im, :] = im
            y = _dot(g_ref[...], yf_ref[i])
            u = read_u(i).astype(F32)
            gate = gate_ref[0, i * blk:(i + 1) * blk, :].astype(F32)
            write(i, gate * (y + u * skip_ref[order:order + 1, :]))

    def write_z(i, val):
        z_ref[i * blk:(i + 1) * blk, :] = val

    def write_o(i, val):
        o_ref[0, i * blk:(i + 1) * blk, :] = val.astype(o_ref.dtype)

    long_conv(lambda j: v_ref[0, j * blk:(j + 1) * blk, :], x1_ref, 0, write_z)
    long_conv(lambda j: z_ref[j * blk:(j + 1) * blk, :], x2_ref, 1, write_o)


def _hyena(p_hy, skip, fmat, gmat, kspec, blk):
    bsz, seq, _ = p_hy.shape
    nb = seq // blk
    nlag = 2 * nb - 1
    nct = D_HY // HY_CT

    def col(off):
        return pl.BlockSpec((1, seq, HY_CT), lambda c, b, off=off: (b, 0, off + c))

    in_specs = [col(0), col(nct), col(2 * nct),
                pl.BlockSpec((2, HY_CT), lambda c, b: (0, c)),
                _resident((2 * blk, blk)), _resident((blk, 2 * blk)),
                pl.BlockSpec((2, nlag, 3, blk, HY_CT), lambda c, b: (0, 0, 0, 0, c),
                             pipeline_mode=pl.Buffered(1))]
    return pl.pallas_call(
        functools.partial(_hy_kernel, seq=seq, blk=blk),
        grid=(nct, bsz),
        in_specs=in_specs,
        out_specs=pl.BlockSpec((1, seq, HY_CT), lambda c, b: (b, 0, c)),
        out_shape=jax.ShapeDtypeStruct((bsz, seq, D_HY), BF16),
        scratch_shapes=[pltpu.VMEM((seq, HY_CT), F32), pltpu.VMEM((nb, 2 * blk, HY_CT), BF16),
                        pltpu.VMEM((nb, 2 * blk, HY_CT), BF16)],
        compiler_params=_cparams(("arbitrary", "arbitrary")),
        name="hyena",
    )(p_hy, p_hy, p_hy, skip, fmat, gmat, kspec)


def _dft_mats(blk):
    n = 2 * blk
    f = lax.broadcasted_iota(jnp.int32, (blk, blk), 0)
    t = lax.broadcasted_iota(jnp.int32, (blk, blk), 1)
    ang = ((f * t) % n).astype(F32) * (2.0 * math.pi / n)
    alt = jnp.where(t % 2 == 0, 1.0, -1.0).astype(F32)
    fre = jnp.cos(ang)
    fim = jnp.where(f == 0, alt, -jnp.sin(ang))
    fmat = jnp.concatenate([fre, fim], axis=0).astype(BF16)
    gre = jnp.where(f == 0, 1.0 / n, (2.0 / n) * jnp.cos(ang)).T
    gim = jnp.where(f == 0, alt / n, -(2.0 / n) * jnp.sin(ang)).T
    gmat = jnp.concatenate([gre, gim], axis=1).astype(BF16)
    return fmat, gmat


def _filter_kernel(z_ref, w1_ref, b1_ref, fr_ref, w2_ref, b2_ref, w3_ref, dl_ref, o_ref):
    z = z_ref[...]
    fr = fr_ref[0]
    a = jnp.sin(fr * (_dot3(z, w1_ref[0]) + b1_ref[0]))
    a = jnp.sin(fr * (_dot3(a, w2_ref[0]) + b2_ref[0]))
    h = _dot3(a, w3_ref[0])
    win = jnp.exp(-z[:, 0:1] * dl_ref[...])
    row = lax.broadcasted_iota(jnp.int32, win.shape, 0)
    win = jnp.where(row == jnp.where(pl.program_id(1) == 0, 0, -1), 0.0, win)
    for order in range(2):
        o_ref[0, order] = h[:, order * D_HY:(order + 1) * D_HY] * win


def _hyena_filters(seq, w1, b1, freq, w2, b2, w3):
    t = jnp.abs(jnp.arange(2 * seq, dtype=F32) - seq)[:, None]
    bands = jnp.arange(1, HY_BANDS + 1, dtype=F32)[None, :]
    ang = (2.0 * math.pi / seq) * t * bands
    z = jnp.concatenate([t / seq, jnp.cos(ang), jnp.sin(ang),
                         jnp.zeros((2 * seq, EMB_PAD - 1 - 2 * HY_BANDS), F32)], axis=-1)
    deltas = jnp.abs(jnp.linspace(math.log(HY_DECAY_TARGET) / HY_SLOW_DECAY,
                                  math.log(HY_DECAY_TARGET) / HY_FAST_DECAY, D_HY, dtype=F32)).reshape(1, D_HY)
    w1p = jnp.pad(w1, ((0, 0), (0, EMB_PAD - w1.shape[1]), (0, 0)))
    fh = w1.shape[-1]
    rt = min(seq, 512)
    nneg = seq // rt

    def per_layer(shape):
        return pl.BlockSpec((1,) + shape, lambda l, r: (l,) + (0,) * len(shape))

    return pl.pallas_call(
        _filter_kernel,
        grid=(DEPTH, 2 * seq // rt),
        in_specs=[pl.BlockSpec((rt, EMB_PAD), lambda l, r: (r, 0)),
                  per_layer((EMB_PAD, fh)), per_layer((1, fh)), per_layer((1, fh)),
                  per_layer((fh, fh)), per_layer((1, fh)),
                  pl.BlockSpec((1, fh, 2 * D_HY), lambda l, r: (l, 0, jnp.where(r < nneg, 1, 0))),
                  pl.BlockSpec((1, D_HY), lambda l, r: (0, 0))],
        out_specs=pl.BlockSpec((1, 2, rt, D_HY), lambda l, r: (l, 0, r, 0)),
        out_shape=jax.ShapeDtypeStruct((DEPTH, 2, 2 * seq, D_HY), F32),
        compiler_params=_cparams(("arbitrary", "arbitrary")),
        name="hyena_filters",
    )(z, w1p, b1.reshape(DEPTH, 1, fh), freq.reshape(DEPTH, 1, fh), w2, b2.reshape(DEPTH, 1, fh), w3, deltas)


def _spec_kernel(kl_ref, kr_ref, fl_ref, fr_ref, o_ref):
    def apply(f_ref, x):
        x_hi, x_lo = _split(x)
        return _dot(f_ref[0], x_hi) + _dot(f_ref[1], x_hi) + _dot(f_ref[0], x_lo)

    res = apply(fl_ref, kl_ref[0, 0]) + apply(fr_ref, kr_ref[0, 0])
    blk = res.shape[0] // 2
    re, im = res[:blk], res[blk:]
    first = lax.broadcasted_iota(jnp.int32, re.shape, 0) == 0
    o_ref[0, 0, 0, 0] = re.astype(o_ref.dtype)
    o_ref[0, 0, 0, 1] = jnp.where(first, 0.0, im).astype(o_ref.dtype)
    o_ref[0, 0, 0, 2] = jnp.where(first, im, re).astype(o_ref.dtype)


def _filter_spectra(kk, seq, blk):
    nb = seq // blk
    nlag = 2 * nb - 1
    n = 2 * blk
    f = lax.broadcasted_iota(jnp.int32, (blk, n), 0)
    q = lax.broadcasted_iota(jnp.int32, (blk, n), 1)
    ang = ((f * q) % n).astype(F32) * (2.0 * math.pi / n)
    sgn = jnp.where(f % 2 == 0, 1.0, -1.0).astype(F32)
    alt = jnp.where(q % 2 == 0, 1.0, -1.0).astype(F32)
    fre = sgn * jnp.cos(ang)
    fim = jnp.where(f == 0, alt, -sgn * jnp.sin(ang))
    fre, fim = jnp.where(q == 0, 0.0, fre), jnp.where(q == 0, 0.0, fim)
    fk = jnp.concatenate([fre, fim], axis=0)
    fk_hi = fk.astype(BF16)
    fk_lo = (fk - fk_hi.astype(F32)).astype(BF16)
    fk2 = jnp.stack([fk_hi, fk_lo])
    return pl.pallas_call(
        _spec_kernel,
        grid=(DEPTH, 2, nlag),
        in_specs=[pl.BlockSpec((1, 1, blk, D_HY), lambda l, o, m: (l, o, m, 0)),
                  pl.BlockSpec((1, 1, blk, D_HY), lambda l, o, m: (l, o, m + 1, 0)),
                  _resident((2, n, blk)), _resident((2, n, blk))],
        out_specs=pl.BlockSpec((1, 1, 1, 3, blk, D_HY), lambda l, o, m: (l, o, m, 0, 0, 0)),
        out_shape=jax.ShapeDtypeStruct((DEPTH, 2, nlag, 3, blk, D_HY), BF16),
        compiler_params=_cparams(("arbitrary", "arbitrary", "arbitrary")),
        name="filter_spectra",
    )(kk, kk, fk2[:, :, :blk], fk2[:, :, blk:])


def _alternate(*gens):
    live = list(gens)
    while live:
        for g in list(live):
            try:
                next(g)
            except StopIteration:
                live.remove(g)


def _gla_kernel(qkvr_ref, g_ref, wa_ref, ba_ref, nw_ref, s0_ref, tri_ref, y_ref, s_ref, o_ref, prep_ref, dec_ref,
                *, seq):
    nblk = seq // GLA_BLOCK
    nchunk = GLA_BLOCK // GLA_CHUNK
    lane = lax.broadcasted_iota(jnp.int32, (1, 2 * GLA_DK), 1)
    head_mask = [(lane < GLA_DK).astype(BF16), (lane >= GLA_DK).astype(BF16)]
    ri = lax.broadcasted_iota(jnp.int32, (GLA_BLOCK, GLA_BLOCK), 0)
    ci = lax.broadcasted_iota(jnp.int32, (GLA_BLOCK, GLA_BLOCK), 1)
    s_ref[...] = s0_ref[...]
    Q_LOC, Q_INT, K_ST, K_STRIP = 0, 1, 2, 3

    def block_rows(bi):
        return pl.ds(pl.multiple_of(bi * GLA_BLOCK, GLA_BLOCK), GLA_BLOCK)

    def prep(d, bi, buf):
        rows = block_rows(bi)
        logit = _dot3(g_ref[0, rows, :], wa_ref[d]) + ba_ref[d]
        yield
        la = (jnp.minimum(logit, 0.0) - jnp.log(1.0 + jnp.exp(-jnp.abs(logit)))) * (1.0 / GLA_GATE_NORM)
        la_hi, la_lo = _split(la)
        yield
        cum_blk = _dot(tri_ref[d, 0], la_hi) + _dot(tri_ref[d, 0], la_lo)
        cum_loc = _dot(tri_ref[d, 1], la_hi) + _dot(tri_ref[d, 1], la_lo)
        yield
        q_loc = qkvr_ref[0, rows, 0:D_GLA_K].astype(F32) * (GLA_DK ** -0.5) * jnp.exp(cum_loc)
        prep_ref[buf, d, Q_LOC] = q_loc.astype(BF16)
        yield
        k_inv = qkvr_ref[0, rows, D_GLA_K:2 * D_GLA_K].astype(F32) * jnp.exp(-cum_loc)
        yield
        edge = GLA_CHUNK - 1 if d == 0 else 0
        firsts = [slice(c * GLA_CHUNK, c * GLA_CHUNK + 1) for c in range(nchunk)]
        bef = [cum_blk[r] - cum_loc[r] for r in firsts]
        tot = [cum_loc[c * GLA_CHUNK + edge:c * GLA_CHUNK + edge + 1] for c in range(nchunk)]
        last = nchunk - 1 if d == 0 else 0
        total = bef[last] + tot[last]
        dec_ref[buf, d, 0:1, :] = jnp.exp(total)
        for c in range(nchunk):
            rc = slice(c * GLA_CHUNK, (c + 1) * GLA_CHUNK)
            prep_ref[buf, d, Q_INT, rc, :] = (q_loc[rc] * jnp.exp(bef[c])).astype(BF16)
            k_end = k_inv[rc] * jnp.exp(tot[c])
            prep_ref[buf, d, K_ST, rc, :] = (k_end * jnp.exp(total - bef[c] - tot[c])).astype(BF16)
            for a in range(nchunk):
                earlier = c < a if d == 0 else c > a
                piece = k_end * jnp.exp(bef[a] - bef[c] - tot[c]) if earlier else k_inv[rc]
                prep_ref[buf, d, K_STRIP + a, rc, :] = piece.astype(BF16)
            yield

    def heads(d, bi, buf):
        rows = block_rows(bi)
        causal = (ci <= ri) if d == 0 else (ci >= ri)
        for h in range(GLA_HEADS):
            pair = slice((h // 2) * 2 * GLA_DK, (h // 2 + 1) * 2 * GLA_DK)
            hm = head_mask[h % 2]
            v_h = qkvr_ref[0, rows, 2 * D_GLA_K + h * GLA_DV:2 * D_GLA_K + (h + 1) * GLA_DV]
            strips = []
            for a in range(nchunk):
                rc = slice(a * GLA_CHUNK, (a + 1) * GLA_CHUNK)
                strips.append(_dot_nt(prep_ref[buf, d, Q_LOC, rc, pair] * hm, prep_ref[buf, d, K_STRIP + a, :, pair]))
            yield
            scores = jnp.where(causal, jnp.concatenate(strips, axis=0), 0.0).astype(BF16)
            st = s_ref[0, d, h]
            o_ref[d, rows, h * GLA_DV:(h + 1) * GLA_DV] = (
                _dot(scores, v_h) + _dot_nt(prep_ref[buf, d, Q_INT, :, pair] * hm, st.astype(BF16)))
            yield
            s_ref[0, d, h] = (dec_ref[buf, d, 0:1, pair] * st
                              + _dot_tn(v_h, prep_ref[buf, d, K_ST, :, pair] * hm))
            yield

    def step(it, buf):
        _alternate(heads(0, it, buf), prep(1, jnp.maximum(nblk - 2 - it, 0), 1 - buf),
                   heads(1, nblk - 1 - it, buf), prep(0, jnp.minimum(it + 1, nblk - 1), 1 - buf))

    _alternate(prep(0, 0, 0), prep(1, nblk - 1, 0))
    if nblk == 1:
        _alternate(heads(0, 0, 0), heads(1, 0, 0))
    else:
        def two_steps(i2, carry):
            step(2 * i2, 0)
            step(2 * i2 + 1, 1)
            return carry
        lax.fori_loop(0, nblk // 2, two_steps, 0)

    for h in range(GLA_HEADS):
        cols = slice(h * GLA_DV, (h + 1) * GLA_DV)
        o = o_ref[0, :, cols] + o_ref[1, :, cols]
        o = o * lax.rsqrt(jnp.mean(o * o, axis=-1, keepdims=True) + RMS_EPS) * nw_ref[...]
        r = qkvr_ref[0, :, 2 * D_GLA_K + D_GLA_V + h * GLA_DV:2 * D_GLA_K + D_GLA_V + (h + 1) * GLA_DV].astype(F32)
        y_ref[0, :, cols] = (o * (r * jax.nn.sigmoid(r))).astype(y_ref.dtype)


def _gla_consts():
    i = np.arange(GLA_BLOCK)
    same = (i[:, None] // GLA_CHUNK) == (i[None, :] // GLA_CHUNK)
    lower = i[None, :] <= i[:, None]
    tri = np.stack([np.stack([lower, lower & same]), np.stack([lower.T, lower.T & same])])
    return jnp.asarray(tri.astype(np.float32), dtype=BF16)


def _gla(qkvr, rank, wa_pad, ba, norm_w, s0):
    bsz, seq, _ = qkvr.shape
    st_shape = (bsz, 2, GLA_HEADS, GLA_DV, 2 * GLA_DK)
    return pl.pallas_call(
        functools.partial(_gla_kernel, seq=seq),
        grid=(bsz,),
        in_specs=[
            pl.BlockSpec((1, seq, GLA_MAIN), lambda b: (b, 0, 0)),
            pl.BlockSpec((1, seq, RANK_PAD), lambda b: (b, 0, 0)),
            _resident((2, RANK_PAD, D_GLA_K)),
            _resident((2, 1, D_GLA_K)),
            _resident((1, GLA_DV)),
            pl.BlockSpec((1,) + st_shape[1:], lambda b: (b, 0, 0, 0, 0)),
            _resident((2, 2, GLA_BLOCK, GLA_BLOCK)),
        ],
        out_specs=[pl.BlockSpec((1, seq, D_GLA_V), lambda b: (b, 0, 0)),
                   pl.BlockSpec((1,) + st_shape[1:], lambda b: (b, 0, 0, 0, 0))],
        out_shape=[jax.ShapeDtypeStruct((bsz, seq, D_GLA_V), BF16), jax.ShapeDtypeStruct(st_shape, F32)],
        scratch_shapes=[pltpu.VMEM((2, seq, D_GLA_V), F32),
                        pltpu.VMEM((2, 2, 3 + GLA_BLOCK // GLA_CHUNK, GLA_BLOCK, D_GLA_K), BF16),
                        pltpu.VMEM((2, 2, 8, D_GLA_K), F32)],
        compiler_params=_cparams(("parallel",)),
        name="gla",
    )(qkvr, rank, wa_pad, ba, norm_w, s0, _gla_consts())


def _pool_kernel(u_ref, a_ref, inv_ref, pw_ref, ps_ref, o_ref, *, seq, width):
    rows_n = seq // width
    nblk = seq // 256
    for g, w in enumerate(POOL_WINDOWS):
        cols = slice(g * POOL_GROUP, (g + 1) * POOL_GROUP)
        ug = u_ref[0, :, cols]
        col = jnp.concatenate([_dot(a_ref[g], ug[b * 256:(b + 1) * 256]) for b in range(nblk)], axis=0)
        col3 = col.reshape(rows_n, width, POOL_GROUP)
        acc = None
        for kk in range(w):
            s = kk - w // 2
            if abs(s) >= rows_n:
                continue
            if s == 0:
                term = col3
            elif s > 0:
                term = jnp.concatenate([col3[s:], jnp.zeros((s, width, POOL_GROUP), F32)], axis=0)
            else:
                term = jnp.concatenate([jnp.zeros((-s, width, POOL_GROUP), F32), col3[:s]], axis=0)
            acc = term if acc is None else acc + term
        mean = acc.reshape(seq, POOL_GROUP) * inv_ref[g]
        dlt = (mean - ug.astype(F32)).astype(BF16)
        o_ref[0, :, cols] = (_dot(dlt, pw_ref[g]) * ps_ref[:, cols]).astype(o_ref.dtype)


def _pool_consts(seq, width):
    rows_n = seq // width
    t = np.arange(256)
    tr, tc = t // width, t % width
    tt = np.arange(seq)
    row, colp = tt // width, tt % width
    mats, invs = [], []
    for w in POOL_WINDOWS:
        lo = np.clip(tc - w // 2, 0, width)
        hi = np.clip(tc - w // 2 + w, 0, width)
        m = (tr[:, None] == tr[None, :]) & (tc[None, :] >= lo[:, None]) & (tc[None, :] < hi[:, None])
        mats.append(m.astype(np.float32))
        cl, ch = np.clip(colp - w // 2, 0, width), np.clip(colp - w // 2 + w, 0, width)
        rl, rh = np.clip(row - w // 2, 0, rows_n), np.clip(row - w // 2 + w, 0, rows_n)
        cnt = ((rh - rl) * (ch - cl)).astype(np.float64)
        invs.append(np.broadcast_to((1.0 / cnt).astype(np.float32)[:, None], (seq, POOL_GROUP)))
    return jnp.asarray(np.stack(mats), dtype=BF16), jnp.asarray(np.stack(invs), dtype=F32)


def _pool(u, pool_w, pool_scale, width):
    bsz, seq, _ = u.shape
    amat, inv = _pool_consts(seq, width)
    ng = len(POOL_WINDOWS)
    return pl.pallas_call(
        functools.partial(_pool_kernel, seq=seq, width=width),
        grid=(bsz,),
        in_specs=[
            pl.BlockSpec((1, seq, D_POOL), lambda b: (b, 0, 0)),
            _resident((ng, 256, 256)),
            _resident((ng, seq, POOL_GROUP)),
            _resident((ng, POOL_GROUP, POOL_GROUP)),
            _resident((1, D_POOL)),
        ],
        out_specs=pl.BlockSpec((1, seq, D_POOL), lambda b: (b, 0, 0)),
        out_shape=jax.ShapeDtypeStruct((bsz, seq, D_POOL), BF16),
        compiler_params=_cparams(("parallel",)),
        name="pool",
    )(u, amat, inv, pool_w, pool_scale)


def _merge_kernel(yh_ref, yg_ref, yp_ref, x_ref, sh_ref, sc_ref, g1_ref, gm_ref, wgate_ref, bgate_ref,
                  wh_ref, wg_ref, wp_ref, wo_ref, o_ref):
    x = x_ref[...]
    h = _ada_norm(x, gm_ref[...], sh_ref[0], sc_ref[0]).astype(BF16)
    m = None
    for k, (y_ref, w_ref) in enumerate(((yh_ref, wh_ref), (yg_ref, wg_ref), (yp_ref, wp_ref))):
        cs = slice(k * D_MODEL, (k + 1) * D_MODEL)
        gate = jax.nn.sigmoid(_dot(h, wgate_ref[:, cs]) + bgate_ref[:, cs])
        term = gate * _dot(y_ref[...], w_ref[...])
        m = term if m is None else m + term
    o_ref[...] = x + g1_ref[0] * _dot(m.astype(BF16), wo_ref[...])


def _merge(y_hy, y_gla, y_pool, x2d, shift, scale, g1, gm, w_gate, b_gate, w_hy, w_gla, w_pool, w_out,
           tokens_per_batch, tm):
    n_tok = x2d.shape[0]
    tpb = tokens_per_batch // tm

    def tok(width):
        return pl.BlockSpec((tm, width), lambda i: (i, 0))

    def per_batch():
        return pl.BlockSpec((1, 1, D_MODEL), lambda i: (i // tpb, 0, 0))

    return pl.pallas_call(
        _merge_kernel,
        grid=(n_tok // tm,),
        in_specs=[tok(D_HY), tok(D_GLA_V), tok(D_POOL), tok(D_MODEL), per_batch(), per_batch(), per_batch(),
                  _resident((1, D_MODEL)), _resident((D_MODEL, 3 * D_MODEL)), _resident((1, 3 * D_MODEL)),
                  _resident((D_HY, D_MODEL)), _resident((D_GLA_V, D_MODEL)), _resident((D_POOL, D_MODEL)),
                  _resident((D_MODEL, D_MODEL))],
        out_specs=tok(D_MODEL),
        out_shape=jax.ShapeDtypeStruct((n_tok, D_MODEL), F32),
        compiler_params=_cparams(("parallel",)),
        name="merge",
    )(y_hy, y_gla, y_pool, x2d, shift, scale, g1, gm, w_gate, b_gate, w_hy, w_gla, w_pool, w_out)


def _mlp_kernel(x_ref, sh_ref, sc_ref, g2_ref, gn_ref, wu_ref, wd_ref, nf_ref, o_ref, *, final_norm):
    x = x_ref[...]
    h = _ada_norm(x, gn_ref[...], sh_ref[0], sc_ref[0]).astype(BF16)
    step = 1024
    acc = jnp.zeros(x.shape, F32)
    for c0 in range(0, D_FF, step):
        u = jnp.maximum(_dot(h, wu_ref[:, c0:c0 + step]), 0.0)
        acc = acc + _dot((u * u).astype(BF16), wd_ref[c0:c0 + step, :])
    y = x + g2_ref[0] * acc
    if final_norm:
        y = y * lax.rsqrt(jnp.mean(y * y, axis=-1, keepdims=True) + RMS_EPS) * nf_ref[...]
    o_ref[...] = y


def _mlp(x2d, shift, scale, g2, gn, w_up, w_down, norm_final, tokens_per_batch, tm, final_norm):
    n_tok = x2d.shape[0]
    tpb = tokens_per_batch // tm

    def per_batch():
        return pl.BlockSpec((1, 1, D_MODEL), lambda i: (i // tpb, 0, 0))

    return pl.pallas_call(
        functools.partial(_mlp_kernel, final_norm=final_norm),
        grid=(n_tok // tm,),
        in_specs=[pl.BlockSpec((tm, D_MODEL), lambda i: (i, 0)), per_batch(), per_batch(), per_batch(),
                  _resident((1, D_MODEL)), _resident((D_MODEL, D_FF)), _resident((D_FF, D_MODEL)),
                  _resident((1, D_MODEL))],
        out_specs=pl.BlockSpec((tm, D_MODEL), lambda i: (i, 0)),
        out_shape=jax.ShapeDtypeStruct((n_tok, D_MODEL), F32),
        compiler_params=_cparams(("parallel",)),
        name="mlp",
    )(x2d, shift, scale, g2, gn, w_up, w_down, norm_final)


SEG_HY = (HY_COLS, BF16, "conv")
SEG_GLA = (GLA_MAIN, BF16, "plain")
SEG_RANK = (RANK_PAD, F32, "plain")
SEG_POOL = (D_POOL, BF16, "plain")


def _hy_block(seq):
    return 512 if seq % 512 == 0 and seq >= 2048 else 256


def _token_tile(seq):
    return 512 if seq % 512 == 0 else 256


def _mixers(pieces, lp, seq, width, s0):
    p_hy, p_gla, p_rank, p_pool = pieces
    n_tok = p_hy.shape[0]
    bsz = n_tok // seq
    blk = _hy_block(seq)
    fmat, gmat = _dft_mats(blk)
    y_hy = _hyena(p_hy.reshape(bsz, seq, HY_COLS), lp["hy_skip"], fmat, gmat, lp["kspec"][seq], blk)
    y_gla, s_fin = _gla(p_gla.reshape(bsz, seq, GLA_MAIN), p_rank.reshape(bsz, seq, RANK_PAD),
                        lp["wa_pad"], lp["ba"], lp["gla_norm_w"], s0)
    y_pool = _pool(p_pool.reshape(bsz, seq, D_POOL), lp["pool_w"], lp["pool_scale"], width)
    return (y_hy.reshape(n_tok, D_HY), y_gla.reshape(n_tok, D_GLA_V), y_pool.reshape(n_tok, D_POOL)), s_fin


def kernel(x, c, ctx, c_ctx, w_mod, b_mod, norm_mix, norm_ffn, w_in, b_in, hy_short_w, hy_short_b, hy_f_w1, hy_f_b1, hy_f_freq, hy_f_w2, hy_f_b2, hy_f_w3, hy_skip, gla_wa_f, gla_ba_f, gla_wa_b, gla_ba_b, gla_norm_w, pool_w, pool_scale, w_br_hy, w_br_gla, w_br_pool, w_out, w_up, w_down, norm_final):
    bsz, seq, _ = x.shape
    ctx_len = ctx.shape[1]
    tm_x, tm_c = _token_tile(seq), _token_tile(ctx_len)

    cc = jnp.concatenate([c, c_ctx[None, :], jnp.zeros((MOD_ROWS - bsz - 1, D_MODEL), F32)], axis=0)
    mod = _modulation(cc, w_mod, b_mod)

    x2d = x.reshape(bsz * seq, D_MODEL)
    c2d = ctx.reshape(bsz * ctx_len, D_MODEL)
    s_zero = jnp.zeros((bsz, 2, GLA_HEADS, GLA_DV, 2 * GLA_DK), F32)
    nf = norm_final.reshape(1, D_MODEL)
    filt_args = (hy_f_w1, hy_f_b1, hy_f_freq, hy_f_w2, hy_f_b2, hy_f_w3)
    kspec_x = _filter_spectra(_hyena_filters(seq, *filt_args), seq, _hy_block(seq))
    kspec_c = _filter_spectra(_hyena_filters(ctx_len, *filt_args), ctx_len, _hy_block(ctx_len))

    for l in range(DEPTH):
        mx = [mod[l, :bsz, i * D_MODEL:(i + 1) * D_MODEL].reshape(bsz, 1, D_MODEL) for i in range(6)]
        mc = [jnp.broadcast_to(mod[l, bsz:bsz + 1, i * D_MODEL:(i + 1) * D_MODEL].reshape(1, 1, D_MODEL),
                               (bsz, 1, D_MODEL)) for i in range(6)]
        wl, bl = w_in[l], b_in[l]
        w_rank = jnp.pad(wl[:, RANK_OFF:POOL_OFF], ((0, 0), (0, RANK_PAD - 2 * GLA_GATE_RANK)))
        b_rank = jnp.pad(bl[RANK_OFF:POOL_OFF], (0, RANK_PAD - 2 * GLA_GATE_RANK))
        seg_w = {"hy": wl[:, :GLA_OFF], "gla": wl[:, GLA_OFF:RANK_OFF], "rank": w_rank,
                 "pool": wl[:, POOL_OFF:GATE_OFF], "gate": wl[:, GATE_OFF:]}
        seg_b = {"hy": bl[:GLA_OFF], "gla": bl[GLA_OFF:RANK_OFF], "rank": b_rank,
                 "pool": bl[POOL_OFF:GATE_OFF], "gate": bl[GATE_OFF:]}
        seg_w = {k: v.astype(BF16) for k, v in seg_w.items()}
        seg_b = {k: v.reshape(1, -1) for k, v in seg_b.items()}

        wa_pad = jnp.zeros((2, RANK_PAD, D_GLA_K), F32)
        wa_pad = wa_pad.at[0, :GLA_GATE_RANK].set(gla_wa_f[l])
        wa_pad = wa_pad.at[1, GLA_GATE_RANK:2 * GLA_GATE_RANK].set(gla_wa_b[l])
        lp = {
            "hy_skip": hy_skip[l],
            "wa_pad": wa_pad, "ba": jnp.stack([gla_ba_f[l], gla_ba_b[l]]).reshape(2, 1, D_GLA_K),
            "gla_norm_w": gla_norm_w[l].reshape(1, GLA_DV),
            "pool_w": pool_w[l].astype(BF16), "pool_scale": pool_scale[l].reshape(1, D_POOL),
            "kspec": {seq: kspec_x[l], ctx_len: kspec_c[l]},
        }
        cw, cb = hy_short_w[l], hy_short_b[l].reshape(1, HY_COLS)
        gm = norm_mix[l].reshape(1, D_MODEL)
        gn = norm_ffn[l].reshape(1, D_MODEL)
        wbh, wbg, wbp = w_br_hy[l].astype(BF16), w_br_gla[l].astype(BF16), w_br_pool[l].astype(BF16)
        wo, wu, wd = w_out[l].astype(BF16), w_up[l].astype(BF16), w_down[l].astype(BF16)
        names = ("hy", "gla", "rank", "pool")
        segs = (SEG_HY, SEG_GLA, SEG_RANK, SEG_POOL)

        if l == DEPTH - 1:
            pc_gla, pc_rank = _in_proj(c2d, mc[0], mc[1], gm, cw, cb, [seg_w["gla"], seg_w["rank"]],
                                       [seg_b["gla"], seg_b["rank"]], (SEG_GLA, SEG_RANK), ctx_len, tm_c)
            _, s_ctx = _gla(pc_gla.reshape(bsz, ctx_len, GLA_MAIN), pc_rank.reshape(bsz, ctx_len, RANK_PAD),
                            lp["wa_pad"], lp["ba"], lp["gla_norm_w"], s_zero)
        else:
            pc = _in_proj(c2d, mc[0], mc[1], gm, cw, cb, [seg_w[n] for n in names], [seg_b[n] for n in names],
                          segs, ctx_len, tm_c)
            ys, s_ctx = _mixers(pc, lp, ctx_len, ctx_len, s_zero)
            c2d = _merge(*ys, c2d, mc[0], mc[1], mc[2], gm, seg_w["gate"], seg_b["gate"], wbh, wbg, wbp, wo,
                         ctx_len, tm_c)
            c2d = _mlp(c2d, mc[3], mc[4], mc[5], gn, wu, wd, nf, ctx_len, tm_c, False)

        px = _in_proj(x2d, mx[0], mx[1], gm, cw, cb, [seg_w[n] for n in names], [seg_b[n] for n in names],
                      segs, seq, tm_x)
        ys, _ = _mixers(px, lp, seq, GRID_W, s_ctx)
        x2d = _merge(*ys, x2d, mx[0], mx[1], mx[2], gm, seg_w["gate"], seg_b["gate"], wbh, wbg, wbp, wo, seq, tm_x)
        x2d = _mlp(x2d, mx[3], mx[4], mx[5], gn, wu, wd, nf, seq, tm_x, l == DEPTH - 1)
    return x2d.reshape(bsz, seq, D_MODEL)
```

```python
import functools
import math

import numpy as np
import jax
import jax.numpy as jnp
from jax import lax
from jax.experimental import pallas as pl
from jax.experimental.pallas import tpu as pltpu

F32 = jnp.float32
BF16 = jnp.bfloat16

D_MODEL = 1024
DEPTH = 2
GRID_W = 64
RMS_EPS = 1e-6

D_HY = 512
HY_BANDS = 8
HY_DECAY_TARGET = 1e-2
HY_FAST_DECAY = 0.3
HY_SLOW_DECAY = 1.5

GLA_HEADS = 4
GLA_DK = 64
GLA_DV = 128
D_GLA_K = GLA_HEADS * GLA_DK
D_GLA_V = GLA_HEADS * GLA_DV
GLA_GATE_RANK = 16
GLA_GATE_NORM = 16.0
GLA_CHUNK = 64
GLA_BLOCK = 256

POOL_WINDOWS = (2, 4, 8, 16)
POOL_GROUP = 128
D_POOL = 512
D_FF = 4 * D_MODEL

HY_COLS = 3 * D_HY
GLA_MAIN = 2 * D_GLA_K + 2 * D_GLA_V
GLA_OFF = HY_COLS
RANK_OFF = GLA_OFF + GLA_MAIN
POOL_OFF = RANK_OFF + 2 * GLA_GATE_RANK
GATE_OFF = POOL_OFF + D_POOL
N_IN = GATE_OFF + 3 * D_MODEL
RANK_PAD = 128

VMEM_LIMIT_V7X = 56 * 1024 * 1024
HY_CT = 256
FREQ_ROWS = 32
MOD_ROWS = 40
EMB_PAD = 128


def _cparams(sem):
    return pltpu.CompilerParams(dimension_semantics=sem, vmem_limit_bytes=VMEM_LIMIT_V7X)


def _dot(a, b):
    return jnp.dot(a, b, preferred_element_type=F32)


def _dot_nt(a, b):
    return lax.dot_general(a, b, (((1,), (1,)), ((), ())), preferred_element_type=F32)


def _dot_tn(a, b):
    return lax.dot_general(a, b, (((0,), (0,)), ((), ())), preferred_element_type=F32)


def _split(a):
    hi = a.astype(BF16)
    lo = (a - hi.astype(F32)).astype(BF16)
    return hi, lo


def _dot3(a, b):
    a_hi, a_lo = _split(a)
    b_hi, b_lo = _split(b)
    return _dot(a_hi, b_hi) + _dot(a_lo, b_hi) + _dot(a_hi, b_lo)


def _resident(shape):
    nd = len(shape)
    return pl.BlockSpec(shape, lambda *_: (0,) * nd, pipeline_mode=pl.Buffered(1))


def _mod_kernel(c_ref, w_ref, b_ref, o_ref):
    c = c_ref[...]
    o_ref[0] = _dot3(c * jax.nn.sigmoid(c), w_ref[0]) + b_ref[0]


def _modulation(cc, w_mod, b_mod):
    tn = 1536
    n = w_mod.shape[-1]
    return pl.pallas_call(
        _mod_kernel,
        grid=(DEPTH, n // tn),
        in_specs=[
            pl.BlockSpec((MOD_ROWS, D_MODEL), lambda l, j: (0, 0)),
            pl.BlockSpec((1, D_MODEL, tn), lambda l, j: (l, 0, j)),
            pl.BlockSpec((1, 1, tn), lambda l, j: (l, 0, j)),
        ],
        out_specs=pl.BlockSpec((1, MOD_ROWS, tn), lambda l, j: (l, 0, j)),
        out_shape=jax.ShapeDtypeStruct((DEPTH, MOD_ROWS, n), F32),
        compiler_params=_cparams(("arbitrary", "arbitrary")),
        name="modulation",
    )(cc, w_mod, b_mod.reshape(DEPTH, 1, n))


def _ada_norm(x, g, shift, scale):
    ms = jnp.mean(x * x, axis=-1, keepdims=True)
    return x * lax.rsqrt(ms + RMS_EPS) * g * (1.0 + scale) + shift


HALO = 8


def _in_kernel(x_ref, xp_ref, xn_ref, sh_ref, sc_ref, g_ref, cw_ref, cb_ref, *refs, segs, tpb):
    ns = len(segs)
    w_refs, b_refs, o_refs = refs[:ns], refs[ns:2 * ns], refs[2 * ns:]
    tm = x_ref.shape[0]
    g, sh, sc = g_ref[...], sh_ref[0], sc_ref[0]
    hm = _ada_norm(x_ref[...], g, sh, sc)
    h = hm.astype(BF16)
    for (width, _, mode), w_ref, b_ref, o_ref in zip(segs, w_refs, b_refs, o_refs):
        step = min(width, 768)
        if mode == "conv":
            h_ext = jnp.concatenate([_ada_norm(xp_ref[...], g, sh, sc), hm, _ada_norm(xn_ref[...], g, sh, sc)],
                                    axis=0).astype(BF16)
            pos = pl.program_id(0) % tpb
            row = lax.broadcasted_iota(jnp.int32, (tm + 2 * HALO, step), 0)
            no_prev = row == jnp.where(pos == 0, HALO, -1)
            no_next = row == jnp.where(pos == tpb - 1, HALO + tm - 1, -1)
        for c0 in range(0, width, step):
            cs = slice(c0, c0 + step)
            if mode == "conv":
                p = _dot(h_ext, w_ref[:, cs]) + b_ref[:, cs]
                prev = jnp.where(no_prev, 0.0, pltpu.roll(p, 1, 0))
                nxt = jnp.where(no_next, 0.0, pltpu.roll(p, tm + 2 * HALO - 1, 0))
                acc = cw_ref[0:1, cs] * prev + cw_ref[1:2, cs] * p + cw_ref[2:3, cs] * nxt + cb_ref[:, cs]
                acc = acc[HALO:HALO + tm]
            else:
                acc = _dot(h, w_ref[:, cs]) + b_ref[:, cs]
            o_ref[:, cs] = acc.astype(o_ref.dtype)


def _in_proj(x2d, shift, scale, g, conv_w, conv_b, weights, biases, segs, tokens_per_batch, tm):
    n_tok = x2d.shape[0]
    tpb = tokens_per_batch // tm
    hb = tm // HALO
    last_halo = n_tok // HALO - 1
    in_specs = [
        pl.BlockSpec((tm, D_MODEL), lambda i: (i, 0)),
        pl.BlockSpec((HALO, D_MODEL), lambda i: (jnp.maximum(i * hb - 1, 0), 0)),
        pl.BlockSpec((HALO, D_MODEL), lambda i: (jnp.minimum((i + 1) * hb, last_halo), 0)),
        pl.BlockSpec((1, 1, D_MODEL), lambda i: (i // tpb, 0, 0)),
        pl.BlockSpec((1, 1, D_MODEL), lambda i: (i // tpb, 0, 0)),
        _resident((1, D_MODEL)),
        _resident(conv_w.shape), _resident(conv_b.shape),
    ]
    in_specs += [_resident((D_MODEL, s[0])) for s in segs]
    in_specs += [_resident((1, s[0])) for s in segs]
    return pl.pallas_call(
        functools.partial(_in_kernel, segs=segs, tpb=tpb),
        grid=(n_tok // tm,),
        in_specs=in_specs,
        out_specs=[pl.BlockSpec((tm, s[0]), lambda i: (i, 0)) for s in segs],
        out_shape=[jax.ShapeDtypeStruct((n_tok, s[0]), s[1]) for s in segs],
        compiler_params=_cparams(("parallel",)),
        name="in_proj",
    )(x2d, x2d, x2d, shift, scale, g, conv_w, conv_b, *weights, *biases)


def _hy_kernel(x1_ref, x2_ref, v_ref, skip_ref, f_ref, g_ref, k_ref, o_ref, z_ref, uf_ref, yf_ref, *, seq, blk):
    nb = seq // blk

    def long_conv(read_u, gate_ref, order, write):
        for j in range(nb):
            uf_ref[j] = _dot(f_ref[...], read_u(j).astype(BF16)).astype(BF16)
        for i in range(nb):
            for r0 in range(0, blk, FREQ_ROWS):
                rows, rows_im = slice(r0, r0 + FREQ_ROWS), slice(blk + r0, blk + r0 + FREQ_ROWS)
                re = im = None
                for j in range(nb):
                    lag = i - j + nb - 1
                    kre, kim, krx = (k_ref[order, lag, s, rows, :] for s in range(3))
                    ure, uim = uf_ref[j, rows, :], uf_ref[j, rows_im, :]
                    t_re = ure * kre - uim * kim
                    t_im = ure * kim + uim * krx
                    re = t_re if re is None else re + t_re
                    im = t_im if im is None else im + t_im
                yf_ref[i, rows, :] = re
                yf_ref[i, rows_im, :] = im
            y = _dot(g_ref[...], yf_ref[i])
            u = read_u(i).astype(F32)
            gate = gate_ref[0, i * blk:(i + 1) * blk, :].astype(F32)
            write(i, gate * (y + u * skip_ref[order:order + 1, :]))

    def write_z(i, val):
        z_ref[i * blk:(i + 1) * blk, :] = val

    def write_o(i, val):
        o_ref[0, i * blk:(i + 1) * blk, :] = val.astype(o_ref.dtype)

    long_conv(lambda j: v_ref[0, j * blk:(j + 1) * blk, :], x1_ref, 0, write_z)
    long_conv(lambda j: z_ref[j * blk:(j + 1) * blk, :], x2_ref, 1, write_o)


def _hyena(p_hy, skip, fmat, gmat, kspec, blk):
    bsz, seq, _ = p_hy.shape
    nb = seq // blk
    nlag = 2 * nb - 1
    nct = D_HY // HY_CT

    def col(off):
        return pl.BlockSpec((1, seq, HY_CT), lambda c, b, off=off: (b, 0, off + c))

    in_specs = [col(0), col(nct), col(2 * nct),
                pl.BlockSpec((2, HY_CT), lambda c, b: (0, c)),
                _resident((2 * blk, blk)), _resident((blk, 2 * blk)),
                pl.BlockSpec((2, nlag, 3, blk, HY_CT), lambda c, b: (0, 0, 0, 0, c),
                             pipeline_mode=pl.Buffered(1))]
    return pl.pallas_call(
        functools.partial(_hy_kernel, seq=seq, blk=blk),
        grid=(nct, bsz),
        in_specs=in_specs,
        out_specs=pl.BlockSpec((1, seq, HY_CT), lambda c, b: (b, 0, c)),
        out_shape=jax.ShapeDtypeStruct((bsz, seq, D_HY), BF16),
        scratch_shapes=[pltpu.VMEM((seq, HY_CT), F32), pltpu.VMEM((nb, 2 * blk, HY_CT), BF16),
                        pltpu.VMEM((nb, 2 * blk, HY_CT), BF16)],
        compiler_params=_cparams(("arbitrary", "arbitrary")),
        name="hyena",
    )(p_hy, p_hy, p_hy, skip, fmat, gmat, kspec)


def _dft_mats(blk):
    n = 2 * blk
    f = lax.broadcasted_iota(jnp.int32, (blk, blk), 0)
    t = lax.broadcasted_iota(jnp.int32, (blk, blk), 1)
    ang = ((f * t) % n).astype(F32) * (2.0 * math.pi / n)
    alt = jnp.where(t % 2 == 0, 1.0, -1.0).astype(F32)
    fre = jnp.cos(ang)
    fim = jnp.where(f == 0, alt, -jnp.sin(ang))
    fmat = jnp.concatenate([fre, fim], axis=0).astype(BF16)
    gre = jnp.where(f == 0, 1.0 / n, (2.0 / n) * jnp.cos(ang)).T
    gim = jnp.where(f == 0, alt / n, -(2.0 / n) * jnp.sin(ang)).T
    gmat = jnp.concatenate([gre, gim], axis=1).astype(BF16)
    return fmat, gmat


def _filter_kernel(z_ref, w1_ref, b1_ref, fr_ref, w2_ref, b2_ref, w3_ref, dl_ref, o_ref):
    z = z_ref[...]
    fr = fr_ref[0]
    a = jnp.sin(fr * (_dot3(z, w1_ref[0]) + b1_ref[0]))
    a = jnp.sin(fr * (_dot3(a, w2_ref[0]) + b2_ref[0]))
    h = _dot3(a, w3_ref[0])
    win = jnp.exp(-z[:, 0:1] * dl_ref[...])
    row = lax.broadcasted_iota(jnp.int32, win.shape, 0)
    win = jnp.where(row == jnp.where(pl.program_id(1) == 0, 0, -1), 0.0, win)
    for order in range(2):
        o_ref[0, order] = h[:, order * D_HY:(order + 1) * D_HY] * win


def _hyena_filters(seq, w1, b1, freq, w2, b2, w3):
    t = jnp.abs(jnp.arange(2 * seq, dtype=F32) - seq)[:, None]
    bands = jnp.arange(1, HY_BANDS + 1, dtype=F32)[None, :]
    ang = (2.0 * math.pi / seq) * t * bands
    z = jnp.concatenate([t / seq, jnp.cos(ang), jnp.sin(ang),
                         jnp.zeros((2 * seq, EMB_PAD - 1 - 2 * HY_BANDS), F32)], axis=-1)
    deltas = jnp.abs(jnp.linspace(math.log(HY_DECAY_TARGET) / HY_SLOW_DECAY,
                                  math.log(HY_DECAY_TARGET) / HY_FAST_DECAY, D_HY, dtype=F32)).reshape(1, D_HY)
    w1p = jnp.pad(w1, ((0, 0), (0, EMB_PAD - w1.shape[1]), (0, 0)))
    fh = w1.shape[-1]
    rt = min(seq, 512)
    nneg = seq // rt

    def per_layer(shape):
        return pl.BlockSpec((1,) + shape, lambda l, r: (l,) + (0,) * len(shape))

    return pl.pallas_call(
        _filter_kernel,
        grid=(DEPTH, 2 * seq // rt),
        in_specs=[pl.BlockSpec((rt, EMB_PAD), lambda l, r: (r, 0)),
                  per_layer((EMB_PAD, fh)), per_layer((1, fh)), per_layer((1, fh)),
                  per_layer((fh, fh)), per_layer((1, fh)),
                  pl.BlockSpec((1, fh, 2 * D_HY), lambda l, r: (l, 0, jnp.where(r < nneg, 1, 0))),
                  pl.BlockSpec((1, D_HY), lambda l, r: (0, 0))],
        out_specs=pl.BlockSpec((1, 2, rt, D_HY), lambda l, r: (l, 0, r, 0)),
        out_shape=jax.ShapeDtypeStruct((DEPTH, 2, 2 * seq, D_HY), F32),
        compiler_params=_cparams(("arbitrary", "arbitrary")),
        name="hyena_filters",
    )(z, w1p, b1.reshape(DEPTH, 1, fh), freq.reshape(DEPTH, 1, fh), w2, b2.reshape(DEPTH, 1, fh), w3, deltas)


def _spec_kernel(kl_ref, kr_ref, fl_ref, fr_ref, o_ref):
    res = _dot(fl_ref[...], kl_ref[0, 0].astype(BF16)) + _dot(fr_ref[...], kr_ref[0, 0].astype(BF16))
    blk = res.shape[0] // 2
    re, im = res[:blk], res[blk:]
    first = lax.broadcasted_iota(jnp.int32, re.shape, 0) == 0
    o_ref[0, 0, 0, 0] = re.astype(o_ref.dtype)
    o_ref[0, 0, 0, 1] = jnp.where(first, 0.0, im).astype(o_ref.dtype)
    o_ref[0, 0, 0, 2] = jnp.where(first, im, re).astype(o_ref.dtype)


def _filter_spectra(kk, seq, blk):
    nb = seq // blk
    nlag = 2 * nb - 1
    n = 2 * blk
    f = lax.broadcasted_iota(jnp.int32, (blk, n), 0)
    q = lax.broadcasted_iota(jnp.int32, (blk, n), 1)
    ang = ((f * q) % n).astype(F32) * (2.0 * math.pi / n)
    sgn = jnp.where(f % 2 == 0, 1.0, -1.0).astype(F32)
    alt = jnp.where(q % 2 == 0, 1.0, -1.0).astype(F32)
    fre = sgn * jnp.cos(ang)
    fim = jnp.where(f == 0, alt, -sgn * jnp.sin(ang))
    fre, fim = jnp.where(q == 0, 0.0, fre), jnp.where(q == 0, 0.0, fim)
    fk = jnp.concatenate([fre, fim], axis=0).astype(BF16)
    return pl.pallas_call(
        _spec_kernel,
        grid=(DEPTH, 2, nlag),
        in_specs=[pl.BlockSpec((1, 1, blk, D_HY), lambda l, o, m: (l, o, m, 0)),
                  pl.BlockSpec((1, 1, blk, D_HY), lambda l, o, m: (l, o, m + 1, 0)),
                  _resident((n, blk)), _resident((n, blk))],
        out_specs=pl.BlockSpec((1, 1, 1, 3, blk, D_HY), lambda l, o, m: (l, o, m, 0, 0, 0)),
        out_shape=jax.ShapeDtypeStruct((DEPTH, 2, nlag, 3, blk, D_HY), BF16),
        compiler_params=_cparams(("arbitrary", "arbitrary", "arbitrary")),
        name="filter_spectra",
    )(kk, kk, fk[:, :blk], fk[:, blk:])


def _alternate(*gens):
    live = list(gens)
    while live:
        for g in list(live):
            try:
                next(g)
            except StopIteration:
                live.remove(g)


def _gla_kernel(qkvr_ref, g_ref, wa_ref, ba_ref, nw_ref, s0_ref, tri_ref, y_ref, s_ref, o_ref, prep_ref, dec_ref,
                *, seq):
    nblk = seq // GLA_BLOCK
    nchunk = GLA_BLOCK // GLA_CHUNK
    lane = lax.broadcasted_iota(jnp.int32, (1, 2 * GLA_DK), 1)
    head_mask = [(lane < GLA_DK).astype(BF16), (lane >= GLA_DK).astype(BF16)]
    ri = lax.broadcasted_iota(jnp.int32, (GLA_BLOCK, GLA_BLOCK), 0)
    ci = lax.broadcasted_iota(jnp.int32, (GLA_BLOCK, GLA_BLOCK), 1)
    s_ref[...] = s0_ref[...]
    Q_LOC, Q_INT, K_ST, K_STRIP = 0, 1, 2, 3

    def block_rows(bi):
        return slice(bi * GLA_BLOCK, (bi + 1) * GLA_BLOCK)

    def prep(d, bi, buf):
        rows = block_rows(bi)
        g = g_ref[0, rows, :]
        g_hi = g.astype(BF16).astype(F32)
        packed = g_hi + pltpu.roll(g - g_hi, 2 * GLA_GATE_RANK, 1) + pltpu.roll(g_hi, 4 * GLA_GATE_RANK, 1)
        logit = _dot(packed.astype(BF16), wa_ref[d]) + ba_ref[d]
        yield
        la = (jnp.minimum(logit, 0.0) - jnp.log(1.0 + jnp.exp(-jnp.abs(logit)))) * (1.0 / GLA_GATE_NORM)
        la_hi, la_lo = _split(la)
        yield
        cum_blk = _dot(tri_ref[d], la_hi) + _dot(tri_ref[d], la_lo)
        zero = jnp.zeros((1, D_GLA_K), F32)
        if d == 0:
            ends = [cum_blk[(c + 1) * GLA_CHUNK - 1:(c + 1) * GLA_CHUNK] for c in range(nchunk)]
            bef = [zero] + ends[:-1]
        else:
            ends = [cum_blk[c * GLA_CHUNK:c * GLA_CHUNK + 1] for c in range(nchunk)]
            bef = ends[1:] + [zero]
        tot = [ends[c] - bef[c] for c in range(nchunk)]
        total = ends[nchunk - 1] if d == 0 else ends[0]
        cum_loc = jnp.concatenate(
            [cum_blk[c * GLA_CHUNK:(c + 1) * GLA_CHUNK] - bef[c] for c in range(nchunk)], axis=0)
        yield
        q_loc = qkvr_ref[0, rows, 0:D_GLA_K].astype(F32) * (GLA_DK ** -0.5) * jnp.exp(cum_loc)
        prep_ref[buf, d, Q_LOC] = q_loc.astype(BF16)
        yield
        k_inv = qkvr_ref[0, rows, D_GLA_K:2 * D_GLA_K].astype(F32) * jnp.exp(-cum_loc)
        yield
        dec_ref[buf, d, 0:1, :] = jnp.exp(total)
        for c in range(nchunk):
            rc = slice(c * GLA_CHUNK, (c + 1) * GLA_CHUNK)
            prep_ref[buf, d, Q_INT, rc, :] = (q_loc[rc] * jnp.exp(bef[c])).astype(BF16)
            k_end = k_inv[rc] * jnp.exp(tot[c])
            prep_ref[buf, d, K_ST, rc, :] = (k_end * jnp.exp(total - bef[c] - tot[c])).astype(BF16)
            for a in range(nchunk):
                earlier = c < a if d == 0 else c > a
                piece = k_end * jnp.exp(bef[a] - bef[c] - tot[c]) if earlier else k_inv[rc]
                prep_ref[buf, d, K_STRIP + a, rc, :] = piece.astype(BF16)
            yield

    def heads(d, bi, buf):
        rows = block_rows(bi)
        causal = (ci <= ri) if d == 0 else (ci >= ri)
        for h in range(GLA_HEADS):
            pair = slice((h // 2) * 2 * GLA_DK, (h // 2 + 1) * 2 * GLA_DK)
            hm = head_mask[h % 2]
            v_h = qkvr_ref[0, rows, 2 * D_GLA_K + h * GLA_DV:2 * D_GLA_K + (h + 1) * GLA_DV]
            strips = []
            for a in range(nchunk):
                rc = slice(a * GLA_CHUNK, (a + 1) * GLA_CHUNK)
                strips.append(_dot_nt(prep_ref[buf, d, Q_LOC, rc, pair] * hm, prep_ref[buf, d, K_STRIP + a, :, pair]))
            yield
            scores = jnp.where(causal, jnp.concatenate(strips, axis=0), 0.0).astype(BF16)
            st = s_ref[0, d, h]
            o_ref[d, rows, h * GLA_DV:(h + 1) * GLA_DV] = (
                _dot(scores, v_h) + _dot_nt(prep_ref[buf, d, Q_INT, :, pair] * hm, st.astype(BF16)))
            yield
            s_ref[0, d, h] = (dec_ref[buf, d, 0:1, pair] * st
                              + _dot_tn(v_h, prep_ref[buf, d, K_ST, :, pair] * hm))
            yield

    def finish(bi):
        rows = slice(bi * GLA_BLOCK, (bi + 1) * GLA_BLOCK)
        for h in range(GLA_HEADS):
            cols = slice(h * GLA_DV, (h + 1) * GLA_DV)
            o = o_ref[0, rows, cols] + o_ref[1, rows, cols]
            o = o * lax.rsqrt(jnp.mean(o * o, axis=-1, keepdims=True) + RMS_EPS) * nw_ref[...]
            r0 = 2 * D_GLA_K + D_GLA_V + h * GLA_DV
            r = qkvr_ref[0, rows, r0:r0 + GLA_DV].astype(F32)
            y_ref[0, rows, cols] = (o * (r * jax.nn.sigmoid(r))).astype(y_ref.dtype)
            yield

    ready = lambda it: [b for b in range(nblk) if max(b, nblk - 1 - b) == it]
    _alternate(prep(0, 0, 0), prep(1, nblk - 1, 0))
    for it in range(nblk):
        buf = it % 2
        streams = [heads(0, it, buf), heads(1, nblk - 1 - it, buf)]
        if it + 1 < nblk:
            streams += [prep(1, nblk - 2 - it, 1 - buf), prep(0, it + 1, 1 - buf)]
        streams += [finish(b) for b in ready(it - 1)]
        _alternate(*streams)
    _alternate(*[finish(b) for b in ready(nblk - 1)])


def _gla_consts():
    i = np.arange(GLA_BLOCK)
    lower = i[None, :] <= i[:, None]
    tri = np.stack([lower, lower.T])
    return jnp.asarray(tri.astype(np.float32), dtype=BF16)


def _gla(qkvr, rank, wa_pad, ba, norm_w, s0):
    bsz, seq, _ = qkvr.shape
    st_shape = (bsz, 2, GLA_HEADS, GLA_DV, 2 * GLA_DK)
    return pl.pallas_call(
        functools.partial(_gla_kernel, seq=seq),
        grid=(bsz,),
        in_specs=[
            pl.BlockSpec((1, seq, GLA_MAIN), lambda b: (b, 0, 0)),
            pl.BlockSpec((1, seq, RANK_PAD), lambda b: (b, 0, 0)),
            _resident((2, RANK_PAD, D_GLA_K)),
            _resident((2, 1, D_GLA_K)),
            _resident((1, GLA_DV)),
            pl.BlockSpec((1,) + st_shape[1:], lambda b: (b, 0, 0, 0, 0)),
            _resident((2, GLA_BLOCK, GLA_BLOCK)),
        ],
        out_specs=[pl.BlockSpec((1, seq, D_GLA_V), lambda b: (b, 0, 0)),
                   pl.BlockSpec((1,) + st_shape[1:], lambda b: (b, 0, 0, 0, 0))],
        out_shape=[jax.ShapeDtypeStruct((bsz, seq, D_GLA_V), BF16), jax.ShapeDtypeStruct(st_shape, F32)],
        scratch_shapes=[pltpu.VMEM((2, seq, D_GLA_V), F32),
                        pltpu.VMEM((2, 2, 3 + GLA_BLOCK // GLA_CHUNK, GLA_BLOCK, D_GLA_K), BF16),
                        pltpu.VMEM((2, 2, 8, D_GLA_K), F32)],
        compiler_params=_cparams(("parallel",)),
        name="gla",
    )(qkvr, rank, wa_pad, ba, norm_w, s0, _gla_consts())


def _pool_kernel(u_ref, a_ref, inv_ref, pw_ref, ps_ref, o_ref, *, seq, width):
    rows_n = seq // width
    nblk = seq // 256
    for g, w in enumerate(POOL_WINDOWS):
        cols = slice(g * POOL_GROUP, (g + 1) * POOL_GROUP)
        ug = u_ref[0, :, cols]
        col = jnp.concatenate([_dot(a_ref[g], ug[b * 256:(b + 1) * 256]) for b in range(nblk)], axis=0)
        col3 = col.reshape(rows_n, width, POOL_GROUP)
        acc = None
        for kk in range(w):
            s = kk - w // 2
            if abs(s) >= rows_n:
                continue
            if s == 0:
                term = col3
            elif s > 0:
                term = jnp.concatenate([col3[s:], jnp.zeros((s, width, POOL_GROUP), F32)], axis=0)
            else:
                term = jnp.concatenate([jnp.zeros((-s, width, POOL_GROUP), F32), col3[:s]], axis=0)
            acc = term if acc is None else acc + term
        mean = acc.reshape(seq, POOL_GROUP) * inv_ref[g]
        dlt = (mean - ug.astype(F32)).astype(BF16)
        o_ref[0, :, cols] = (_dot(dlt, pw_ref[g]) * ps_ref[:, cols]).astype(o_ref.dtype)


def _pool_consts(seq, width):
    rows_n = seq // width
    t = np.arange(256)
    tr, tc = t // width, t % width
    tt = np.arange(seq)
    row, colp = tt // width, tt % width
    mats, invs = [], []
    for w in POOL_WINDOWS:
        lo = np.clip(tc - w // 2, 0, width)
        hi = np.clip(tc - w // 2 + w, 0, width)
        m = (tr[:, None] == tr[None, :]) & (tc[None, :] >= lo[:, None]) & (tc[None, :] < hi[:, None])
        mats.append(m.astype(np.float32))
        cl, ch = np.clip(colp - w // 2, 0, width), np.clip(colp - w // 2 + w, 0, width)
        rl, rh = np.clip(row - w // 2, 0, rows_n), np.clip(row - w // 2 + w, 0, rows_n)
        cnt = ((rh - rl) * (ch - cl)).astype(np.float64)
        invs.append(np.broadcast_to((1.0 / cnt).astype(np.float32)[:, None], (seq, POOL_GROUP)))
    return jnp.asarray(np.stack(mats), dtype=BF16), jnp.asarray(np.stack(invs), dtype=F32)


def _pool(u, pool_w, pool_scale, width):
    bsz, seq, _ = u.shape
    amat, inv = _pool_consts(seq, width)
    ng = len(POOL_WINDOWS)
    return pl.pallas_call(
        functools.partial(_pool_kernel, seq=seq, width=width),
        grid=(bsz,),
        in_specs=[
            pl.BlockSpec((1, seq, D_POOL), lambda b: (b, 0, 0)),
            _resident((ng, 256, 256)),
            _resident((ng, seq, POOL_GROUP)),
            _resident((ng, POOL_GROUP, POOL_GROUP)),
            _resident((1, D_POOL)),
        ],
        out_specs=pl.BlockSpec((1, seq, D_POOL), lambda b: (b, 0, 0)),
        out_shape=jax.ShapeDtypeStruct((bsz, seq, D_POOL), BF16),
        compiler_params=_cparams(("parallel",)),
        name="pool",
    )(u, amat, inv, pool_w, pool_scale)


def _merge_kernel(yh_ref, yg_ref, yp_ref, x_ref, sh_ref, sc_ref, g1_ref, gm_ref, wgate_ref, bgate_ref,
                  wh_ref, wg_ref, wp_ref, wo_ref, o_ref):
    x = x_ref[...]
    h = _ada_norm(x, gm_ref[...], sh_ref[0], sc_ref[0]).astype(BF16)
    m = None
    for k, (y_ref, w_ref) in enumerate(((yh_ref, wh_ref), (yg_ref, wg_ref), (yp_ref, wp_ref))):
        cs = slice(k * D_MODEL, (k + 1) * D_MODEL)
        gate = jax.nn.sigmoid(_dot(h, wgate_ref[:, cs]) + bgate_ref[:, cs])
        term = gate * _dot(y_ref[...], w_ref[...])
        m = term if m is None else m + term
    o_ref[...] = x + g1_ref[0] * _dot(m.astype(BF16), wo_ref[...])


def _merge(y_hy, y_gla, y_pool, x2d, shift, scale, g1, gm, w_gate, b_gate, w_hy, w_gla, w_pool, w_out,
           tokens_per_batch, tm):
    n_tok = x2d.shape[0]
    tpb = tokens_per_batch // tm

    def tok(width):
        return pl.BlockSpec((tm, width), lambda i: (i, 0))

    def per_batch():
        return pl.BlockSpec((1, 1, D_MODEL), lambda i: (i // tpb, 0, 0))

    return pl.pallas_call(
        _merge_kernel,
        grid=(n_tok // tm,),
        in_specs=[tok(D_HY), tok(D_GLA_V), tok(D_POOL), tok(D_MODEL), per_batch(), per_batch(), per_batch(),
                  _resident((1, D_MODEL)), _resident((D_MODEL, 3 * D_MODEL)), _resident((1, 3 * D_MODEL)),
                  _resident((D_HY, D_MODEL)), _resident((D_GLA_V, D_MODEL)), _resident((D_POOL, D_MODEL)),
                  _resident((D_MODEL, D_MODEL))],
        out_specs=tok(D_MODEL),
        out_shape=jax.ShapeDtypeStruct((n_tok, D_MODEL), F32),
        compiler_params=_cparams(("parallel",)),
        name="merge",
    )(y_hy, y_gla, y_pool, x2d, shift, scale, g1, gm, w_gate, b_gate, w_hy, w_gla, w_pool, w_out)


def _mlp_kernel(x_ref, sh_ref, sc_ref, g2_ref, gn_ref, wu_ref, wd_ref, nf_ref, o_ref, *, final_norm):
    x = x_ref[...]
    h = _ada_norm(x, gn_ref[...], sh_ref[0], sc_ref[0]).astype(BF16)
    step = 1024
    acc = jnp.zeros(x.shape, F32)
    for c0 in range(0, D_FF, step):
        u = jnp.maximum(_dot(h, wu_ref[:, c0:c0 + step]), 0.0)
        acc = acc + _dot((u * u).astype(BF16), wd_ref[c0:c0 + step, :])
    y = x + g2_ref[0] * acc
    if final_norm:
        y = y * lax.rsqrt(jnp.mean(y * y, axis=-1, keepdims=True) + RMS_EPS) * nf_ref[...]
    o_ref[...] = y


def _mlp(x2d, shift, scale, g2, gn, w_up, w_down, norm_final, tokens_per_batch, tm, final_norm):
    n_tok = x2d.shape[0]
    tpb = tokens_per_batch // tm

    def per_batch():
        return pl.BlockSpec((1, 1, D_MODEL), lambda i: (i // tpb, 0, 0))

    return pl.pallas_call(
        functools.partial(_mlp_kernel, final_norm=final_norm),
        grid=(n_tok // tm,),
        in_specs=[pl.BlockSpec((tm, D_MODEL), lambda i: (i, 0)), per_batch(), per_batch(), per_batch(),
                  _resident((1, D_MODEL)), _resident((D_MODEL, D_FF)), _resident((D_FF, D_MODEL)),
                  _resident((1, D_MODEL))],
        out_specs=pl.BlockSpec((tm, D_MODEL), lambda i: (i, 0)),
        out_shape=jax.ShapeDtypeStruct((n_tok, D_MODEL), F32),
        compiler_params=_cparams(("parallel",)),
        name="mlp",
    )(x2d, shift, scale, g2, gn, w_up, w_down, norm_final)


SEG_HY = (HY_COLS, BF16, "conv")
SEG_GLA = (GLA_MAIN, BF16, "plain")
SEG_RANK = (RANK_PAD, F32, "plain")
SEG_POOL = (D_POOL, BF16, "plain")


def _hy_block(seq):
    return 512 if seq % 512 == 0 and seq >= 2048 else 256


def _token_tile(seq):
    for tm in (1024, 512):
        if seq % tm == 0:
            return tm
    return 256


def _mixers(pieces, lp, seq, width, s0):
    p_hy, p_gla, p_rank, p_pool = pieces
    n_tok = p_hy.shape[0]
    bsz = n_tok // seq
    blk = _hy_block(seq)
    fmat, gmat = _dft_mats(blk)
    y_hy = _hyena(p_hy.reshape(bsz, seq, HY_COLS), lp["hy_skip"], fmat, gmat, lp["kspec"][seq], blk)
    y_gla, s_fin = _gla(p_gla.reshape(bsz, seq, GLA_MAIN), p_rank.reshape(bsz, seq, RANK_PAD),
                        lp["wa_pad"], lp["ba"], lp["gla_norm_w"], s0)
    y_pool = _pool(p_pool.reshape(bsz, seq, D_POOL), lp["pool_w"], lp["pool_scale"], width)
    return (y_hy.reshape(n_tok, D_HY), y_gla.reshape(n_tok, D_GLA_V), y_pool.reshape(n_tok, D_POOL)), s_fin


def kernel(x, c, ctx, c_ctx, w_mod, b_mod, norm_mix, norm_ffn, w_in, b_in, hy_short_w, hy_short_b, hy_f_w1, hy_f_b1, hy_f_freq, hy_f_w2, hy_f_b2, hy_f_w3, hy_skip, gla_wa_f, gla_ba_f, gla_wa_b, gla_ba_b, gla_norm_w, pool_w, pool_scale, w_br_hy, w_br_gla, w_br_pool, w_out, w_up, w_down, norm_final):
    bsz, seq, _ = x.shape
    ctx_len = ctx.shape[1]
    tm_x, tm_c = _token_tile(seq), _token_tile(ctx_len)

    cc = jnp.concatenate([c, c_ctx[None, :], jnp.zeros((MOD_ROWS - bsz - 1, D_MODEL), F32)], axis=0)
    mod = _modulation(cc, w_mod, b_mod)

    x2d = x.reshape(bsz * seq, D_MODEL)
    c2d = ctx.reshape(bsz * ctx_len, D_MODEL)
    s_zero = jnp.zeros((bsz, 2, GLA_HEADS, GLA_DV, 2 * GLA_DK), F32)
    nf = norm_final.reshape(1, D_MODEL)
    filt_args = (hy_f_w1, hy_f_b1, hy_f_freq, hy_f_w2, hy_f_b2, hy_f_w3)
    kspec_x = _filter_spectra(_hyena_filters(seq, *filt_args), seq, _hy_block(seq))
    kspec_c = _filter_spectra(_hyena_filters(ctx_len, *filt_args), ctx_len, _hy_block(ctx_len))

    for l in range(DEPTH):
        mx = [mod[l, :bsz, i * D_MODEL:(i + 1) * D_MODEL].reshape(bsz, 1, D_MODEL) for i in range(6)]
        mc = [jnp.broadcast_to(mod[l, bsz:bsz + 1, i * D_MODEL:(i + 1) * D_MODEL].reshape(1, 1, D_MODEL),
                               (bsz, 1, D_MODEL)) for i in range(6)]
        wl, bl = w_in[l], b_in[l]
        w_rank = jnp.pad(wl[:, RANK_OFF:POOL_OFF], ((0, 0), (0, RANK_PAD - 2 * GLA_GATE_RANK)))
        b_rank = jnp.pad(bl[RANK_OFF:POOL_OFF], (0, RANK_PAD - 2 * GLA_GATE_RANK))
        seg_w = {"hy": wl[:, :GLA_OFF], "gla": wl[:, GLA_OFF:RANK_OFF], "rank": w_rank,
                 "pool": wl[:, POOL_OFF:GATE_OFF], "gate": wl[:, GATE_OFF:]}
        seg_b = {"hy": bl[:GLA_OFF], "gla": bl[GLA_OFF:RANK_OFF], "rank": b_rank,
                 "pool": bl[POOL_OFF:GATE_OFF], "gate": bl[GATE_OFF:]}
        seg_w = {k: v.astype(BF16) for k, v in seg_w.items()}
        seg_b = {k: v.reshape(1, -1) for k, v in seg_b.items()}

        zr = jnp.zeros((GLA_GATE_RANK, D_GLA_K), F32)
        wa = jnp.stack([jnp.concatenate([gla_wa_f[l], zr]), jnp.concatenate([zr, gla_wa_b[l]])])
        wa_hi = wa.astype(BF16)
        wa_lo = (wa - wa_hi.astype(F32)).astype(BF16)
        wa_pad = jnp.concatenate([wa_hi, wa_hi, wa_lo, jnp.zeros((2, RANK_PAD - 6 * GLA_GATE_RANK, D_GLA_K), BF16)],
                                 axis=1)
        lp = {
            "hy_skip": hy_skip[l],
            "wa_pad": wa_pad, "ba": jnp.stack([gla_ba_f[l], gla_ba_b[l]]).reshape(2, 1, D_GLA_K),
            "gla_norm_w": gla_norm_w[l].reshape(1, GLA_DV),
            "pool_w": pool_w[l].astype(BF16), "pool_scale": pool_scale[l].reshape(1, D_POOL),
            "kspec": {seq: kspec_x[l], ctx_len: kspec_c[l]},
        }
        cw, cb = hy_short_w[l], hy_short_b[l].reshape(1, HY_COLS)
        gm = norm_mix[l].reshape(1, D_MODEL)
        gn = norm_ffn[l].reshape(1, D_MODEL)
        wbh, wbg, wbp = w_br_hy[l].astype(BF16), w_br_gla[l].astype(BF16), w_br_pool[l].astype(BF16)
        wo, wu, wd = w_out[l].astype(BF16), w_up[l].astype(BF16), w_down[l].astype(BF16)
        names = ("hy", "gla", "rank", "pool")
        segs = (SEG_HY, SEG_GLA, SEG_RANK, SEG_POOL)

        if l == DEPTH - 1:
            pc_gla, pc_rank = _in_proj(c2d, mc[0], mc[1], gm, cw, cb, [seg_w["gla"], seg_w["rank"]],
                                       [seg_b["gla"], seg_b["rank"]], (SEG_GLA, SEG_RANK), ctx_len, tm_c)
            _, s_ctx = _gla(pc_gla.reshape(bsz, ctx_len, GLA_MAIN), pc_rank.reshape(bsz, ctx_len, RANK_PAD),
                            lp["wa_pad"], lp["ba"], lp["gla_norm_w"], s_zero)
        else:
            pc = _in_proj(c2d, mc[0], mc[1], gm, cw, cb, [seg_w[n] for n in names], [seg_b[n] for n in names],
                          segs, ctx_len, tm_c)
            ys, s_ctx = _mixers(pc, lp, ctx_len, ctx_len, s_zero)
            c2d = _merge(*ys, c2d, mc[0], mc[1], mc[2], gm, seg_w["gate"], seg_b["gate"], wbh, wbg, wbp, wo,
                         ctx_len, tm_c)
            c2d = _mlp(c2d, mc[3], mc[4], mc[5], gn, wu, wd, nf, ctx_len, tm_c, False)

        px = _in_proj(x2d, mx[0], mx[1], gm, cw, cb, [seg_w[n] for n in names], [seg_b[n] for n in names],
                      segs, seq, tm_x)
        ys, _ = _mixers(px, lp, seq, GRID_W, s_ctx)
        x2d = _merge(*ys, x2d, mx[0], mx[1], mx[2], gm, seg_w["gate"], seg_b["gate"], wbh, wbg, wbp, wo, seq, tm_x)
        x2d = _mlp(x2d, mx[3], mx[4], mx[5], gn, wu, wd, nf, seq, tm_x, l == DEPTH - 1)
    return x2d.reshape(bsz, seq, D_MODEL)
```

```python
import functools
import math

import numpy as np
import jax
import jax.numpy as jnp
from jax import lax
from jax.experimental import pallas as pl
from jax.experimental.pallas import tpu as pltpu

F32 = jnp.float32
BF16 = jnp.bfloat16

D_MODEL = 1024
DEPTH = 2
GRID_W = 64
RMS_EPS = 1e-6

D_HY = 512
HY_BANDS = 8
HY_DECAY_TARGET = 1e-2
HY_FAST_DECAY = 0.3
HY_SLOW_DECAY = 1.5

GLA_HEADS = 4
GLA_DK = 64
GLA_DV = 128
D_GLA_K = GLA_HEADS * GLA_DK
D_GLA_V = GLA_HEADS * GLA_DV
GLA_GATE_RANK = 16
GLA_GATE_NORM = 16.0
GLA_CHUNK = 64
GLA_BLOCK = 256

POOL_WINDOWS = (2, 4, 8, 16)
POOL_GROUP = 128
D_POOL = 512
D_FF = 4 * D_MODEL

HY_COLS = 3 * D_HY
GLA_MAIN = 2 * D_GLA_K + 2 * D_GLA_V
GLA_OFF = HY_COLS
RANK_OFF = GLA_OFF + GLA_MAIN
POOL_OFF = RANK_OFF + 2 * GLA_GATE_RANK
GATE_OFF = POOL_OFF + D_POOL
N_IN = GATE_OFF + 3 * D_MODEL
RANK_PAD = 128

VMEM_LIMIT_V7X = 56 * 1024 * 1024
HY_CT = 256
FREQ_ROWS = 32
MOD_ROWS = 40
EMB_PAD = 128


def _cparams(sem):
    return pltpu.CompilerParams(dimension_semantics=sem, vmem_limit_bytes=VMEM_LIMIT_V7X)


def _dot(a, b):
    return jnp.dot(a, b, preferred_element_type=F32)


def _dot_nt(a, b):
    return lax.dot_general(a, b, (((1,), (1,)), ((), ())), preferred_element_type=F32)


def _dot_tn(a, b):
    return lax.dot_general(a, b, (((0,), (0,)), ((), ())), preferred_element_type=F32)


def _split(a):
    hi = a.astype(BF16)
    lo = (a - hi.astype(F32)).astype(BF16)
    return hi, lo


def _dot3(a, b):
    a_hi, a_lo = _split(a)
    b_hi, b_lo = _split(b)
    return _dot(a_hi, b_hi) + _dot(a_lo, b_hi) + _dot(a_hi, b_lo)


def _resident(shape):
    nd = len(shape)
    return pl.BlockSpec(shape, lambda *_: (0,) * nd, pipeline_mode=pl.Buffered(1))


def _mod_kernel(c_ref, w_ref, b_ref, o_ref):
    c = c_ref[...]
    o_ref[0] = _dot3(c * jax.nn.sigmoid(c), w_ref[0]) + b_ref[0]


def _modulation(cc, w_mod, b_mod):
    tn = 1536
    n = w_mod.shape[-1]
    return pl.pallas_call(
        _mod_kernel,
        grid=(DEPTH, n // tn),
        in_specs=[
            pl.BlockSpec((MOD_ROWS, D_MODEL), lambda l, j: (0, 0)),
            pl.BlockSpec((1, D_MODEL, tn), lambda l, j: (l, 0, j)),
            pl.BlockSpec((1, 1, tn), lambda l, j: (l, 0, j)),
        ],
        out_specs=pl.BlockSpec((1, MOD_ROWS, tn), lambda l, j: (l, 0, j)),
        out_shape=jax.ShapeDtypeStruct((DEPTH, MOD_ROWS, n), F32),
        compiler_params=_cparams(("arbitrary", "arbitrary")),
        name="modulation",
    )(cc, w_mod, b_mod.reshape(DEPTH, 1, n))


def _ada_norm(x, g, shift, scale):
    ms = jnp.mean(x * x, axis=-1, keepdims=True)
    return x * lax.rsqrt(ms + RMS_EPS) * g * (1.0 + scale) + shift


HALO = 8


def _batch_spec(tokens_per_batch, tm):
    return pl.BlockSpec((1, 1, D_MODEL), lambda i: ((i * tm) // tokens_per_batch, 0, 0))


def _in_kernel(x_ref, xp_ref, xn_ref, sh_ref, sc_ref, g_ref, cw_ref, cb_ref, *refs, segs, seq_len):
    ns = len(segs)
    w_refs, b_refs, o_refs = refs[:ns], refs[ns:2 * ns], refs[2 * ns:]
    tm = x_ref.shape[0]
    g, sh, sc = g_ref[...], sh_ref[0], sc_ref[0]
    hm = _ada_norm(x_ref[...], g, sh, sc)
    h = hm.astype(BF16)
    for (width, _, mode), w_ref, b_ref, o_ref in zip(segs, w_refs, b_refs, o_refs):
        step = min(width, 768)
        if mode == "conv":
            h_ext = jnp.concatenate([_ada_norm(xp_ref[...], g, sh, sc), hm, _ada_norm(xn_ref[...], g, sh, sc)],
                                    axis=0).astype(BF16)
            row = lax.broadcasted_iota(jnp.int32, (tm + 2 * HALO, step), 0)
            if seq_len >= tm:
                tpb = seq_len // tm
                pos = pl.program_id(0) % tpb
                no_prev = row == jnp.where(pos == 0, HALO, -1)
                no_next = row == jnp.where(pos == tpb - 1, HALO + tm - 1, -1)
            else:
                no_prev = functools.reduce(jnp.logical_or, [row == HALO + s for s in range(0, tm, seq_len)])
                no_next = functools.reduce(jnp.logical_or,
                                           [row == HALO + s + seq_len - 1 for s in range(0, tm, seq_len)])
        for c0 in range(0, width, step):
            cs = slice(c0, c0 + step)
            if mode == "conv":
                p = _dot(h_ext, w_ref[:, cs]) + b_ref[:, cs]
                prev = jnp.where(no_prev, 0.0, pltpu.roll(p, 1, 0))
                nxt = jnp.where(no_next, 0.0, pltpu.roll(p, tm + 2 * HALO - 1, 0))
                acc = cw_ref[0:1, cs] * prev + cw_ref[1:2, cs] * p + cw_ref[2:3, cs] * nxt + cb_ref[:, cs]
                acc = acc[HALO:HALO + tm]
            else:
                acc = _dot(h, w_ref[:, cs]) + b_ref[:, cs]
            o_ref[:, cs] = acc.astype(o_ref.dtype)


def _in_proj(x2d, shift, scale, g, conv_w, conv_b, weights, biases, segs, tokens_per_batch, tm):
    n_tok = x2d.shape[0]
    hb = tm // HALO
    last_halo = n_tok // HALO - 1
    in_specs = [
        pl.BlockSpec((tm, D_MODEL), lambda i: (i, 0)),
        pl.BlockSpec((HALO, D_MODEL), lambda i: (jnp.maximum(i * hb - 1, 0), 0)),
        pl.BlockSpec((HALO, D_MODEL), lambda i: (jnp.minimum((i + 1) * hb, last_halo), 0)),
        _batch_spec(tokens_per_batch, tm), _batch_spec(tokens_per_batch, tm),
        _resident((1, D_MODEL)),
        _resident(conv_w.shape), _resident(conv_b.shape),
    ]
    in_specs += [_resident((D_MODEL, s[0])) for s in segs]
    in_specs += [_resident((1, s[0])) for s in segs]
    return pl.pallas_call(
        functools.partial(_in_kernel, segs=segs, seq_len=tokens_per_batch),
        grid=(n_tok // tm,),
        in_specs=in_specs,
        out_specs=[pl.BlockSpec((tm, s[0]), lambda i: (i, 0)) for s in segs],
        out_shape=[jax.ShapeDtypeStruct((n_tok, s[0]), s[1]) for s in segs],
        compiler_params=_cparams(("parallel",)),
        name="in_proj",
    )(x2d, x2d, x2d, shift, scale, g, conv_w, conv_b, *weights, *biases)


def _hy_kernel(x1_ref, x2_ref, v_ref, skip_ref, f_ref, g_ref, k_ref, o_ref, z_ref, uf_ref, yf_ref, *, seq, blk):
    nb = seq // blk

    def long_conv(read_u, gate_ref, order, write):
        for j in range(nb):
            uf_ref[j] = _dot(f_ref[...], read_u(j).astype(BF16)).astype(BF16)
        for i in range(nb):
            for r0 in range(0, blk, FREQ_ROWS):
                rows, rows_im = slice(r0, r0 + FREQ_ROWS), slice(blk + r0, blk + r0 + FREQ_ROWS)
                re = im = None
                for j in range(nb):
                    lag = i - j + nb - 1
                    kre, kim, krx = (k_ref[order, lag, s, rows, :] for s in range(3))
                    ure, uim = uf_ref[j, rows, :], uf_ref[j, rows_im, :]
                    t_re = ure * kre - uim * kim
                    t_im = ure * kim + uim * krx
                    re = t_re if re is None else re + t_re
                    im = t_im if im is None else im + t_im
                yf_ref[i, rows, :] = re
                yf_ref[i, rows_im, :] = im
            y = _dot(g_ref[...], yf_ref[i])
            u = read_u(i).astype(F32)
            gate = gate_ref[0, i * blk:(i + 1) * blk, :].astype(F32)
            write(i, gate * (y + u * skip_ref[order:order + 1, :]))

    def write_z(i, val):
        z_ref[i * blk:(i + 1) * blk, :] = val

    def write_o(i, val):
        o_ref[0, i * blk:(i + 1) * blk, :] = val.astype(o_ref.dtype)

    long_conv(lambda j: v_ref[0, j * blk:(j + 1) * blk, :], x1_ref, 0, write_z)
    long_conv(lambda j: z_ref[j * blk:(j + 1) * blk, :], x2_ref, 1, write_o)


def _hyena(p_hy, skip, fmat, gmat, kspec, blk):
    bsz, seq, _ = p_hy.shape
    nb = seq // blk
    nlag = 2 * nb - 1
    nct = D_HY // HY_CT

    def col(off):
        return pl.BlockSpec((1, seq, HY_CT), lambda c, b, off=off: (b, 0, off + c))

    in_specs = [col(0), col(nct), col(2 * nct),
                pl.BlockSpec((2, HY_CT), lambda c, b: (0, c)),
                _resident((2 * blk, blk)), _resident((blk, 2 * blk)),
                pl.BlockSpec((2, nlag, 3, blk, HY_CT), lambda c, b: (0, 0, 0, 0, c),
                             pipeline_mode=pl.Buffered(1))]
    return pl.pallas_call(
        functools.partial(_hy_kernel, seq=seq, blk=blk),
        grid=(nct, bsz),
        in_specs=in_specs,
        out_specs=pl.BlockSpec((1, seq, HY_CT), lambda c, b: (b, 0, c)),
        out_shape=jax.ShapeDtypeStruct((bsz, seq, D_HY), BF16),
        scratch_shapes=[pltpu.VMEM((seq, HY_CT), F32), pltpu.VMEM((nb, 2 * blk, HY_CT), BF16),
                        pltpu.VMEM((nb, 2 * blk, HY_CT), BF16)],
        compiler_params=_cparams(("arbitrary", "arbitrary")),
        name="hyena",
    )(p_hy, p_hy, p_hy, skip, fmat, gmat, kspec)


def _dft_mats(blk):
    n = 2 * blk
    f = np.arange(blk)[:, None]
    t = np.arange(blk)[None, :]
    ang = ((f * t) % n) * (2.0 * math.pi / n)
    alt = np.where(t % 2 == 0, 1.0, -1.0)
    fre = np.cos(ang)
    fim = np.where(f == 0, alt, -np.sin(ang))
    fmat = np.concatenate([fre, fim], axis=0)
    gre = np.where(f == 0, 1.0 / n, (2.0 / n) * np.cos(ang)).T
    gim = np.where(f == 0, alt / n, -(2.0 / n) * np.sin(ang)).T
    gmat = np.concatenate([gre, gim], axis=1)
    return jnp.asarray(fmat.astype(BF16)), jnp.asarray(gmat.astype(BF16))


def _filter_kernel(z_ref, w1_ref, b1_ref, fr_ref, w2_ref, b2_ref, w3_ref, dl_ref, o_ref):
    z = z_ref[...]
    fr = fr_ref[0]
    a = jnp.sin(fr * (_dot3(z, w1_ref[0]) + b1_ref[0]))
    a = jnp.sin(fr * (_dot3(a, w2_ref[0]) + b2_ref[0]))
    h = _dot3(a, w3_ref[0])
    win = jnp.exp(-z[:, 0:1] * dl_ref[...])
    row = lax.broadcasted_iota(jnp.int32, win.shape, 0)
    win = jnp.where(row == jnp.where(pl.program_id(1) == 0, 0, -1), 0.0, win)
    for order in range(2):
        o_ref[0, order] = h[:, order * D_HY:(order + 1) * D_HY] * win


def _hyena_filters(seq, w1, b1, freq, w2, b2, w3):
    t = np.abs(np.arange(2 * seq, dtype=np.float64) - seq)[:, None]
    ang = (2.0 * math.pi / seq) * t * np.arange(1, HY_BANDS + 1, dtype=np.float64)[None, :]
    z = jnp.asarray(np.concatenate([t / seq, np.cos(ang), np.sin(ang),
                                    np.zeros((2 * seq, EMB_PAD - 1 - 2 * HY_BANDS))], axis=-1), dtype=F32)
    deltas = jnp.asarray(np.abs(np.linspace(math.log(HY_DECAY_TARGET) / HY_SLOW_DECAY,
                                            math.log(HY_DECAY_TARGET) / HY_FAST_DECAY, D_HY)).reshape(1, D_HY),
                         dtype=F32)
    w1p = jnp.pad(w1, ((0, 0), (0, EMB_PAD - w1.shape[1]), (0, 0)))
    fh = w1.shape[-1]
    rt = min(seq, 512)
    nneg = seq // rt

    def per_layer(shape):
        return pl.BlockSpec((1,) + shape, lambda l, r: (l,) + (0,) * len(shape))

    return pl.pallas_call(
        _filter_kernel,
        grid=(DEPTH, 2 * seq // rt),
        in_specs=[pl.BlockSpec((rt, EMB_PAD), lambda l, r: (r, 0)),
                  per_layer((EMB_PAD, fh)), per_layer((1, fh)), per_layer((1, fh)),
                  per_layer((fh, fh)), per_layer((1, fh)),
                  pl.BlockSpec((1, fh, 2 * D_HY), lambda l, r: (l, 0, jnp.where(r < nneg, 1, 0))),
                  pl.BlockSpec((1, D_HY), lambda l, r: (0, 0))],
        out_specs=pl.BlockSpec((1, 2, rt, D_HY), lambda l, r: (l, 0, r, 0)),
        out_shape=jax.ShapeDtypeStruct((DEPTH, 2, 2 * seq, D_HY), F32),
        compiler_params=_cparams(("arbitrary", "arbitrary")),
        name="hyena_filters",
    )(z, w1p, b1.reshape(DEPTH, 1, fh), freq.reshape(DEPTH, 1, fh), w2, b2.reshape(DEPTH, 1, fh), w3, deltas)


def _spec_kernel(kl_ref, kr_ref, fl_ref, fr_ref, o_ref):
    res = _dot(fl_ref[...], kl_ref[0, 0].astype(BF16)) + _dot(fr_ref[...], kr_ref[0, 0].astype(BF16))
    blk = res.shape[0] // 2
    re, im = res[:blk], res[blk:]
    first = lax.broadcasted_iota(jnp.int32, re.shape, 0) == 0
    o_ref[0, 0, 0, 0] = re.astype(o_ref.dtype)
    o_ref[0, 0, 0, 1] = jnp.where(first, 0.0, im).astype(o_ref.dtype)
    o_ref[0, 0, 0, 2] = jnp.where(first, im, re).astype(o_ref.dtype)


def _filter_spectra(kk, seq, blk):
    nb = seq // blk
    nlag = 2 * nb - 1
    n = 2 * blk
    f = np.arange(blk)[:, None]
    q = np.arange(n)[None, :]
    ang = ((f * q) % n) * (2.0 * math.pi / n)
    sgn = np.where(f % 2 == 0, 1.0, -1.0)
    alt = np.where(q % 2 == 0, 1.0, -1.0)
    fre = sgn * np.cos(ang)
    fim = np.where(f == 0, alt, -sgn * np.sin(ang))
    fre, fim = np.where(q == 0, 0.0, fre), np.where(q == 0, 0.0, fim)
    fk = jnp.asarray(np.concatenate([fre, fim], axis=0).astype(BF16))
    return pl.pallas_call(
        _spec_kernel,
        grid=(DEPTH, 2, nlag),
        in_specs=[pl.BlockSpec((1, 1, blk, D_HY), lambda l, o, m: (l, o, m, 0)),
                  pl.BlockSpec((1, 1, blk, D_HY), lambda l, o, m: (l, o, m + 1, 0)),
                  _resident((n, blk)), _resident((n, blk))],
        out_specs=pl.BlockSpec((1, 1, 1, 3, blk, D_HY), lambda l, o, m: (l, o, m, 0, 0, 0)),
        out_shape=jax.ShapeDtypeStruct((DEPTH, 2, nlag, 3, blk, D_HY), BF16),
        compiler_params=_cparams(("arbitrary", "arbitrary", "arbitrary")),
        name="filter_spectra",
    )(kk, kk, fk[:, :blk], fk[:, blk:])


def _alternate(*gens):
    live = list(gens)
    while live:
        for g in list(live):
            try:
                next(g)
            except StopIteration:
                live.remove(g)


def _gla_kernel(qkvr_ref, g_ref, wa_ref, ba_ref, nw_ref, s0_ref, tri_ref, y_ref, s_ref, o_ref, prep_ref, dec_ref,
                *, seq):
    nblk = seq // GLA_BLOCK
    nchunk = GLA_BLOCK // GLA_CHUNK
    lane = lax.broadcasted_iota(jnp.int32, (1, 2 * GLA_DK), 1)
    head_mask = [(lane < GLA_DK).astype(BF16), (lane >= GLA_DK).astype(BF16)]
    ri = lax.broadcasted_iota(jnp.int32, (GLA_BLOCK, GLA_BLOCK), 0)
    ci = lax.broadcasted_iota(jnp.int32, (GLA_BLOCK, GLA_BLOCK), 1)
    s_ref[...] = s0_ref[...]
    Q_LOC, Q_INT, K_ST, K_STRIP = 0, 1, 2, 3

    def block_rows(bi):
        return slice(bi * GLA_BLOCK, (bi + 1) * GLA_BLOCK)

    def prep(d, bi, buf):
        rows = block_rows(bi)
        g = g_ref[0, rows, :]
        g_hi = g.astype(BF16).astype(F32)
        packed = g_hi + pltpu.roll(g - g_hi, 2 * GLA_GATE_RANK, 1) + pltpu.roll(g_hi, 4 * GLA_GATE_RANK, 1)
        logit = _dot(packed.astype(BF16), wa_ref[d]) + ba_ref[d]
        yield
        la = (jnp.minimum(logit, 0.0) - jnp.log(1.0 + jnp.exp(-jnp.abs(logit)))) * (1.0 / GLA_GATE_NORM)
        la_hi, la_lo = _split(la)
        yield
        cum_blk = _dot(tri_ref[d], la_hi) + _dot(tri_ref[d], la_lo)
        zero = jnp.zeros((1, D_GLA_K), F32)
        if d == 0:
            ends = [cum_blk[(c + 1) * GLA_CHUNK - 1:(c + 1) * GLA_CHUNK] for c in range(nchunk)]
            bef = [zero] + ends[:-1]
        else:
            ends = [cum_blk[c * GLA_CHUNK:c * GLA_CHUNK + 1] for c in range(nchunk)]
            bef = ends[1:] + [zero]
        tot = [ends[c] - bef[c] for c in range(nchunk)]
        total = ends[nchunk - 1] if d == 0 else ends[0]
        cum_loc = jnp.concatenate(
            [cum_blk[c * GLA_CHUNK:(c + 1) * GLA_CHUNK] - bef[c] for c in range(nchunk)], axis=0)
        yield
        q_loc = qkvr_ref[0, rows, 0:D_GLA_K].astype(F32) * (GLA_DK ** -0.5) * jnp.exp(cum_loc)
        prep_ref[buf, d, Q_LOC] = q_loc.astype(BF16)
        yield
        k_inv = qkvr_ref[0, rows, D_GLA_K:2 * D_GLA_K].astype(F32) * jnp.exp(-cum_loc)
        yield
        dec_ref[buf, d, 0:1, :] = jnp.exp(total)
        for c in range(nchunk):
            rc = slice(c * GLA_CHUNK, (c + 1) * GLA_CHUNK)
            prep_ref[buf, d, Q_INT, rc, :] = (q_loc[rc] * jnp.exp(bef[c])).astype(BF16)
            k_end = k_inv[rc] * jnp.exp(tot[c])
            prep_ref[buf, d, K_ST, rc, :] = (k_end * jnp.exp(total - bef[c] - tot[c])).astype(BF16)
            for a in range(nchunk):
                earlier = c < a if d == 0 else c > a
                piece = k_end * jnp.exp(bef[a] - bef[c] - tot[c]) if earlier else k_inv[rc]
                prep_ref[buf, d, K_STRIP + a, rc, :] = piece.astype(BF16)
            yield

    def heads(d, bi, buf):
        rows = block_rows(bi)
        causal = (ci <= ri) if d == 0 else (ci >= ri)
        for h in range(GLA_HEADS):
            pair = slice((h // 2) * 2 * GLA_DK, (h // 2 + 1) * 2 * GLA_DK)
            hm = head_mask[h % 2]
            v_h = qkvr_ref[0, rows, 2 * D_GLA_K + h * GLA_DV:2 * D_GLA_K + (h + 1) * GLA_DV]
            strips = []
            for a in range(nchunk):
                rc = slice(a * GLA_CHUNK, (a + 1) * GLA_CHUNK)
                strips.append(_dot_nt(prep_ref[buf, d, Q_LOC, rc, pair] * hm, prep_ref[buf, d, K_STRIP + a, :, pair]))
            yield
            scores = jnp.where(causal, jnp.concatenate(strips, axis=0), 0.0).astype(BF16)
            st = s_ref[0, d, h]
            o_ref[d, rows, h * GLA_DV:(h + 1) * GLA_DV] = (
                _dot(scores, v_h) + _dot_nt(prep_ref[buf, d, Q_INT, :, pair] * hm, st.astype(BF16)))
            yield
            s_ref[0, d, h] = (dec_ref[buf, d, 0:1, pair] * st
                              + _dot_tn(v_h, prep_ref[buf, d, K_ST, :, pair] * hm))
            yield

    def finish(bi):
        rows = slice(bi * GLA_BLOCK, (bi + 1) * GLA_BLOCK)
        for h in range(GLA_HEADS):
            cols = slice(h * GLA_DV, (h + 1) * GLA_DV)
            o = o_ref[0, rows, cols] + o_ref[1, rows, cols]
            o = o * lax.rsqrt(jnp.mean(o * o, axis=-1, keepdims=True) + RMS_EPS) * nw_ref[...]
            r0 = 2 * D_GLA_K + D_GLA_V + h * GLA_DV
            r = qkvr_ref[0, rows, r0:r0 + GLA_DV].astype(F32)
            y_ref[0, rows, cols] = (o * (r * jax.nn.sigmoid(r))).astype(y_ref.dtype)
            yield

    ready = lambda it: [b for b in range(nblk) if max(b, nblk - 1 - b) == it]
    _alternate(prep(0, 0, 0), prep(1, nblk - 1, 0))
    for it in range(nblk):
        buf = it % 2
        streams = [heads(0, it, buf), heads(1, nblk - 1 - it, buf)]
        if it + 1 < nblk:
            streams += [prep(1, nblk - 2 - it, 1 - buf), prep(0, it + 1, 1 - buf)]
        streams += [finish(b) for b in ready(it - 1)]
        _alternate(*streams)
    _alternate(*[finish(b) for b in ready(nblk - 1)])


def _gla_consts():
    i = np.arange(GLA_BLOCK)
    lower = i[None, :] <= i[:, None]
    tri = np.stack([lower, lower.T])
    return jnp.asarray(tri.astype(np.float32), dtype=BF16)


def _gla(qkvr, rank, wa_pad, ba, norm_w, s0):
    bsz, seq, _ = qkvr.shape
    st_shape = (bsz, 2, GLA_HEADS, GLA_DV, 2 * GLA_DK)
    return pl.pallas_call(
        functools.partial(_gla_kernel, seq=seq),
        grid=(bsz,),
        in_specs=[
            pl.BlockSpec((1, seq, GLA_MAIN), lambda b: (b, 0, 0)),
            pl.BlockSpec((1, seq, RANK_PAD), lambda b: (b, 0, 0)),
            _resident((2, RANK_PAD, D_GLA_K)),
            _resident((2, 1, D_GLA_K)),
            _resident((1, GLA_DV)),
            pl.BlockSpec((1,) + st_shape[1:], lambda b: (b, 0, 0, 0, 0)),
            _resident((2, GLA_BLOCK, GLA_BLOCK)),
        ],
        out_specs=[pl.BlockSpec((1, seq, D_GLA_V), lambda b: (b, 0, 0)),
                   pl.BlockSpec((1,) + st_shape[1:], lambda b: (b, 0, 0, 0, 0))],
        out_shape=[jax.ShapeDtypeStruct((bsz, seq, D_GLA_V), BF16), jax.ShapeDtypeStruct(st_shape, F32)],
        scratch_shapes=[pltpu.VMEM((2, seq, D_GLA_V), F32),
                        pltpu.VMEM((2, 2, 3 + GLA_BLOCK // GLA_CHUNK, GLA_BLOCK, D_GLA_K), BF16),
                        pltpu.VMEM((2, 2, 8, D_GLA_K), F32)],
        compiler_params=_cparams(("parallel",)),
        name="gla",
    )(qkvr, rank, wa_pad, ba, norm_w, s0, _gla_consts())


def _pool_kernel(u_ref, a_ref, inv_ref, pw_ref, ps_ref, o_ref, *, seq, width):
    rows_n = seq // width
    nblk = seq // 256
    for g, w in enumerate(POOL_WINDOWS):
        cols = slice(g * POOL_GROUP, (g + 1) * POOL_GROUP)
        ug = u_ref[0, :, cols]
        col = jnp.concatenate([_dot(a_ref[g], ug[b * 256:(b + 1) * 256]) for b in range(nblk)], axis=0)
        col3 = col.reshape(rows_n, width, POOL_GROUP)
        acc = None
        for kk in range(w):
            s = kk - w // 2
            if abs(s) >= rows_n:
                continue
            if s == 0:
                term = col3
            elif s > 0:
                term = jnp.concatenate([col3[s:], jnp.zeros((s, width, POOL_GROUP), F32)], axis=0)
            else:
                term = jnp.concatenate([jnp.zeros((-s, width, POOL_GROUP), F32), col3[:s]], axis=0)
            acc = term if acc is None else acc + term
        mean = acc.reshape(seq, POOL_GROUP) * inv_ref[g]
        dlt = (mean - ug.astype(F32)).astype(BF16)
        o_ref[0, :, cols] = (_dot(dlt, pw_ref[g]) * ps_ref[:, cols]).astype(o_ref.dtype)


def _pool_consts(seq, width):
    rows_n = seq // width
    t = np.arange(256)
    tr, tc = t // width, t % width
    tt = np.arange(seq)
    row, colp = tt // width, tt % width
    mats, invs = [], []
    for w in POOL_WINDOWS:
        lo = np.clip(tc - w // 2, 0, width)
        hi = np.clip(tc - w // 2 + w, 0, width)
        m = (tr[:, None] == tr[None, :]) & (tc[None, :] >= lo[:, None]) & (tc[None, :] < hi[:, None])
        mats.append(m.astype(np.float32))
        cl, ch = np.clip(colp - w // 2, 0, width), np.clip(colp - w // 2 + w, 0, width)
        rl, rh = np.clip(row - w // 2, 0, rows_n), np.clip(row - w // 2 + w, 0, rows_n)
        cnt = ((rh - rl) * (ch - cl)).astype(np.float64)
        invs.append(np.broadcast_to((1.0 / cnt).astype(np.float32)[:, None], (seq, POOL_GROUP)))
    return jnp.asarray(np.stack(mats), dtype=BF16), jnp.asarray(np.stack(invs), dtype=F32)


def _pool(u, pool_w, pool_scale, width):
    bsz, seq, _ = u.shape
    amat, inv = _pool_consts(seq, width)
    ng = len(POOL_WINDOWS)
    return pl.pallas_call(
        functools.partial(_pool_kernel, seq=seq, width=width),
        grid=(bsz,),
        in_specs=[
            pl.BlockSpec((1, seq, D_POOL), lambda b: (b, 0, 0)),
            _resident((ng, 256, 256)),
            _resident((ng, seq, POOL_GROUP)),
            _resident((ng, POOL_GROUP, POOL_GROUP)),
            _resident((1, D_POOL)),
        ],
        out_specs=pl.BlockSpec((1, seq, D_POOL), lambda b: (b, 0, 0)),
        out_shape=jax.ShapeDtypeStruct((bsz, seq, D_POOL), BF16),
        compiler_params=_cparams(("parallel",)),
        name="pool",
    )(u, amat, inv, pool_w, pool_scale)


def _merge_kernel(yh_ref, yg_ref, yp_ref, x_ref, sh_ref, sc_ref, g1_ref, gm_ref, wgate_ref, bgate_ref,
                  wh_ref, wg_ref, wp_ref, wo_ref, o_ref):
    x = x_ref[...]
    h = _ada_norm(x, gm_ref[...], sh_ref[0], sc_ref[0]).astype(BF16)
    m = None
    for k, (y_ref, w_ref) in enumerate(((yh_ref, wh_ref), (yg_ref, wg_ref), (yp_ref, wp_ref))):
        cs = slice(k * D_MODEL, (k + 1) * D_MODEL)
        gate = jax.nn.sigmoid(_dot(h, wgate_ref[:, cs]) + bgate_ref[:, cs])
        term = gate * _dot(y_ref[...], w_ref[...])
        m = term if m is None else m + term
    o_ref[...] = x + g1_ref[0] * _dot(m.astype(BF16), wo_ref[...])


def _merge(y_hy, y_gla, y_pool, x2d, shift, scale, g1, gm, w_gate, b_gate, w_hy, w_gla, w_pool, w_out,
           tokens_per_batch, tm):
    n_tok = x2d.shape[0]

    def tok(width):
        return pl.BlockSpec((tm, width), lambda i: (i, 0))

    def per_batch():
        return _batch_spec(tokens_per_batch, tm)

    return pl.pallas_call(
        _merge_kernel,
        grid=(n_tok // tm,),
        in_specs=[tok(D_HY), tok(D_GLA_V), tok(D_POOL), tok(D_MODEL), per_batch(), per_batch(), per_batch(),
                  _resident((1, D_MODEL)), _resident((D_MODEL, 3 * D_MODEL)), _resident((1, 3 * D_MODEL)),
                  _resident((D_HY, D_MODEL)), _resident((D_GLA_V, D_MODEL)), _resident((D_POOL, D_MODEL)),
                  _resident((D_MODEL, D_MODEL))],
        out_specs=tok(D_MODEL),
        out_shape=jax.ShapeDtypeStruct((n_tok, D_MODEL), F32),
        compiler_params=_cparams(("parallel",)),
        name="merge",
    )(y_hy, y_gla, y_pool, x2d, shift, scale, g1, gm, w_gate, b_gate, w_hy, w_gla, w_pool, w_out)


def _mlp_kernel(x_ref, sh_ref, sc_ref, g2_ref, gn_ref, wu_ref, wd_ref, nf_ref, o_ref, *, final_norm):
    x = x_ref[...]
    h = _ada_norm(x, gn_ref[...], sh_ref[0], sc_ref[0]).astype(BF16)
    step = 1024
    acc = jnp.zeros(x.shape, F32)
    for c0 in range(0, D_FF, step):
        u = jnp.maximum(_dot(h, wu_ref[:, c0:c0 + step]), 0.0)
        acc = acc + _dot((u * u).astype(BF16), wd_ref[c0:c0 + step, :])
    y = x + g2_ref[0] * acc
    if final_norm:
        y = y * lax.rsqrt(jnp.mean(y * y, axis=-1, keepdims=True) + RMS_EPS) * nf_ref[...]
    o_ref[...] = y


def _mlp(x2d, shift, scale, g2, gn, w_up, w_down, norm_final, tokens_per_batch, tm, final_norm):
    n_tok = x2d.shape[0]

    def per_batch():
        return _batch_spec(tokens_per_batch, tm)

    return pl.pallas_call(
        functools.partial(_mlp_kernel, final_norm=final_norm),
        grid=(n_tok // tm,),
        in_specs=[pl.BlockSpec((tm, D_MODEL), lambda i: (i, 0)), per_batch(), per_batch(), per_batch(),
                  _resident((1, D_MODEL)), _resident((D_MODEL, D_FF)), _resident((D_FF, D_MODEL)),
                  _resident((1, D_MODEL))],
        out_specs=pl.BlockSpec((tm, D_MODEL), lambda i: (i, 0)),
        out_shape=jax.ShapeDtypeStruct((n_tok, D_MODEL), F32),
        compiler_params=_cparams(("parallel",)),
        name="mlp",
    )(x2d, shift, scale, g2, gn, w_up, w_down, norm_final)


SEG_HY = (HY_COLS, BF16, "conv")
SEG_GLA = (GLA_MAIN, BF16, "plain")
SEG_RANK = (RANK_PAD, F32, "plain")
SEG_POOL = (D_POOL, BF16, "plain")


def _hy_block(seq):
    return 512 if seq % 512 == 0 and seq >= 2048 else 256


def _token_tile(seq):
    for tm in (1024, 512):
        if seq % tm == 0:
            return tm
    return 256


def _mixers(pieces, lp, seq, width, s0):
    p_gla, p_hy, p_rank, p_pool = pieces
    n_tok = p_hy.shape[0]
    bsz = n_tok // seq
    blk = _hy_block(seq)
    fmat, gmat = _dft_mats(blk)
    y_hy = _hyena(p_hy.reshape(bsz, seq, HY_COLS), lp["hy_skip"], fmat, gmat, lp["kspec"][seq], blk)
    y_gla, s_fin = _gla(p_gla.reshape(bsz, seq, GLA_MAIN), p_rank.reshape(bsz, seq, RANK_PAD),
                        lp["wa_pad"], lp["ba"], lp["gla_norm_w"], s0)
    y_pool = _pool(p_pool.reshape(bsz, seq, D_POOL), lp["pool_w"], lp["pool_scale"], width)
    return (y_hy.reshape(n_tok, D_HY), y_gla.reshape(n_tok, D_GLA_V), y_pool.reshape(n_tok, D_POOL)), s_fin


def kernel(x, c, ctx, c_ctx, w_mod, b_mod, norm_mix, norm_ffn, w_in, b_in, hy_short_w, hy_short_b, hy_f_w1, hy_f_b1, hy_f_freq, hy_f_w2, hy_f_b2, hy_f_w3, hy_skip, gla_wa_f, gla_ba_f, gla_wa_b, gla_ba_b, gla_norm_w, pool_w, pool_scale, w_br_hy, w_br_gla, w_br_pool, w_out, w_up, w_down, norm_final):
    bsz, seq, _ = x.shape
    ctx_len = ctx.shape[1]
    tm_x, tm_c = _token_tile(seq), _token_tile(bsz * ctx_len)
    assert tm_c % ctx_len == 0 or ctx_len % tm_c == 0

    cc = jnp.concatenate([c, c_ctx[None, :], jnp.zeros((MOD_ROWS - bsz - 1, D_MODEL), F32)], axis=0)
    mod = _modulation(cc, w_mod, b_mod)

    x2d = x.reshape(bsz * seq, D_MODEL)
    c2d = ctx.reshape(bsz * ctx_len, D_MODEL)
    s_zero = jnp.zeros((bsz, 2, GLA_HEADS, GLA_DV, 2 * GLA_DK), F32)
    nf = norm_final.reshape(1, D_MODEL)
    filt_args = (hy_f_w1, hy_f_b1, hy_f_freq, hy_f_w2, hy_f_b2, hy_f_w3)
    kspec_x = _filter_spectra(_hyena_filters(seq, *filt_args), seq, _hy_block(seq))
    kspec_c = _filter_spectra(_hyena_filters(ctx_len, *filt_args), ctx_len, _hy_block(ctx_len))

    for l in range(DEPTH):
        mx = [mod[l, :bsz, i * D_MODEL:(i + 1) * D_MODEL].reshape(bsz, 1, D_MODEL) for i in range(6)]
        mc = [jnp.broadcast_to(mod[l, bsz:bsz + 1, i * D_MODEL:(i + 1) * D_MODEL].reshape(1, 1, D_MODEL),
                               (bsz, 1, D_MODEL)) for i in range(6)]
        wl, bl = w_in[l], b_in[l]
        w_rank = jnp.pad(wl[:, RANK_OFF:POOL_OFF], ((0, 0), (0, RANK_PAD - 2 * GLA_GATE_RANK)))
        b_rank = jnp.pad(bl[RANK_OFF:POOL_OFF], (0, RANK_PAD - 2 * GLA_GATE_RANK))
        seg_w = {"hy": wl[:, :GLA_OFF], "gla": wl[:, GLA_OFF:RANK_OFF], "rank": w_rank,
                 "pool": wl[:, POOL_OFF:GATE_OFF], "gate": wl[:, GATE_OFF:]}
        seg_b = {"hy": bl[:GLA_OFF], "gla": bl[GLA_OFF:RANK_OFF], "rank": b_rank,
                 "pool": bl[POOL_OFF:GATE_OFF], "gate": bl[GATE_OFF:]}
        seg_w = {k: v.astype(BF16) for k, v in seg_w.items()}
        seg_b = {k: v.reshape(1, -1) for k, v in seg_b.items()}

        zr = jnp.zeros((GLA_GATE_RANK, D_GLA_K), F32)
        wa = jnp.stack([jnp.concatenate([gla_wa_f[l], zr]), jnp.concatenate([zr, gla_wa_b[l]])])
        wa_hi = wa.astype(BF16)
        wa_lo = (wa - wa_hi.astype(F32)).astype(BF16)
        wa_pad = jnp.concatenate([wa_hi, wa_hi, wa_lo, jnp.zeros((2, RANK_PAD - 6 * GLA_GATE_RANK, D_GLA_K), BF16)],
                                 axis=1)
        lp = {
            "hy_skip": hy_skip[l],
            "wa_pad": wa_pad, "ba": jnp.stack([gla_ba_f[l], gla_ba_b[l]]).reshape(2, 1, D_GLA_K),
            "gla_norm_w": gla_norm_w[l].reshape(1, GLA_DV),
            "pool_w": pool_w[l].astype(BF16), "pool_scale": pool_scale[l].reshape(1, D_POOL),
            "kspec": {seq: kspec_x[l], ctx_len: kspec_c[l]},
        }
        cw, cb = hy_short_w[l], hy_short_b[l].reshape(1, HY_COLS)
        gm = norm_mix[l].reshape(1, D_MODEL)
        gn = norm_ffn[l].reshape(1, D_MODEL)
        wbh, wbg, wbp = w_br_hy[l].astype(BF16), w_br_gla[l].astype(BF16), w_br_pool[l].astype(BF16)
        wo, wu, wd = w_out[l].astype(BF16), w_up[l].astype(BF16), w_down[l].astype(BF16)
        names = ("gla", "hy", "rank", "pool")
        segs = (SEG_GLA, SEG_HY, SEG_RANK, SEG_POOL)

        if l == DEPTH - 1:
            pc_gla, pc_rank = _in_proj(c2d, mc[0], mc[1], gm, cw, cb, [seg_w["gla"], seg_w["rank"]],
                                       [seg_b["gla"], seg_b["rank"]], (SEG_GLA, SEG_RANK), ctx_len, tm_c)
            _, s_ctx = _gla(pc_gla.reshape(bsz, ctx_len, GLA_MAIN), pc_rank.reshape(bsz, ctx_len, RANK_PAD),
                            lp["wa_pad"], lp["ba"], lp["gla_norm_w"], s_zero)
        else:
            pc = _in_proj(c2d, mc[0], mc[1], gm, cw, cb, [seg_w[n] for n in names], [seg_b[n] for n in names],
                          segs, ctx_len, tm_c)
            ys, s_ctx = _mixers(pc, lp, ctx_len, ctx_len, s_zero)
            c2d = _merge(*ys, c2d, mc[0], mc[1], mc[2], gm, seg_w["gate"], seg_b["gate"], wbh, wbg, wbp, wo,
                         ctx_len, tm_c)
            c2d = _mlp(c2d, mc[3], mc[4], mc[5], gn, wu, wd, nf, ctx_len, tm_c, False)

        px = _in_proj(x2d, mx[0], mx[1], gm, cw, cb, [seg_w[n] for n in names], [seg_b[n] for n in names],
                      segs, seq, tm_x)
        ys, _ = _mixers(px, lp, seq, GRID_W, s_ctx)
        x2d = _merge(*ys, x2d, mx[0], mx[1], mx[2], gm, seg_w["gate"], seg_b["gate"], wbh, wbg, wbp, wo, seq, tm_x)
        x2d = _mlp(x2d, mx[3], mx[4], mx[5], gn, wu, wd, nf, seq, tm_x, l == DEPTH - 1)
    return x2d.reshape(bsz, seq, D_MODEL)
```

```python
import functools
import math

import numpy as np
import jax
import jax.numpy as jnp
from jax import lax
from jax.experimental import pallas as pl
from jax.experimental.pallas import tpu as pltpu

F32 = jnp.float32
BF16 = jnp.bfloat16

D_MODEL = 1024
DEPTH = 2
GRID_W = 64
RMS_EPS = 1e-6

D_HY = 512
HY_BANDS = 8
HY_DECAY_TARGET = 1e-2
HY_FAST_DECAY = 0.3
HY_SLOW_DECAY = 1.5

GLA_HEADS = 4
GLA_DK = 64
GLA_DV = 128
D_GLA_K = GLA_HEADS * GLA_DK
D_GLA_V = GLA_HEADS * GLA_DV
GLA_GATE_RANK = 16
GLA_GATE_NORM = 16.0
GLA_CHUNK = 64
GLA_BLOCK = 256

POOL_WINDOWS = (2, 4, 8, 16)
POOL_GROUP = 128
D_POOL = 512
D_FF = 4 * D_MODEL

HY_COLS = 3 * D_HY
GLA_MAIN = 2 * D_GLA_K + 2 * D_GLA_V
GLA_OFF = HY_COLS
RANK_OFF = GLA_OFF + GLA_MAIN
POOL_OFF = RANK_OFF + 2 * GLA_GATE_RANK
GATE_OFF = POOL_OFF + D_POOL
N_IN = GATE_OFF + 3 * D_MODEL
RANK_PAD = 128

VMEM_LIMIT_V7X = 56 * 1024 * 1024
HY_CT = 256
FREQ_ROWS = 32
MOD_ROWS = 40
EMB_PAD = 128


def _cparams(sem):
    return pltpu.CompilerParams(dimension_semantics=sem, vmem_limit_bytes=VMEM_LIMIT_V7X)


def _dot(a, b):
    return jnp.dot(a, b, preferred_element_type=F32)


def _dot_nt(a, b):
    return lax.dot_general(a, b, (((1,), (1,)), ((), ())), preferred_element_type=F32)


def _dot_tn(a, b):
    return lax.dot_general(a, b, (((0,), (0,)), ((), ())), preferred_element_type=F32)


def _split(a):
    hi = a.astype(BF16)
    lo = (a - hi.astype(F32)).astype(BF16)
    return hi, lo


def _dot3(a, b):
    a_hi, a_lo = _split(a)
    b_hi, b_lo = _split(b)
    return _dot(a_hi, b_hi) + _dot(a_lo, b_hi) + _dot(a_hi, b_lo)


def _resident(shape):
    nd = len(shape)
    return pl.BlockSpec(shape, lambda *_: (0,) * nd, pipeline_mode=pl.Buffered(1))


def _mod_kernel(c_ref, w_ref, b_ref, o_ref):
    c = c_ref[...]
    o_ref[0] = _dot3(c * jax.nn.sigmoid(c), w_ref[0]) + b_ref[0]


def _modulation(cc, w_mod, b_mod):
    tn = 1536
    n = w_mod.shape[-1]
    return pl.pallas_call(
        _mod_kernel,
        grid=(DEPTH, n // tn),
        in_specs=[
            pl.BlockSpec((MOD_ROWS, D_MODEL), lambda l, j: (0, 0)),
            pl.BlockSpec((1, D_MODEL, tn), lambda l, j: (l, 0, j)),
            pl.BlockSpec((1, 1, tn), lambda l, j: (l, 0, j)),
        ],
        out_specs=pl.BlockSpec((1, MOD_ROWS, tn), lambda l, j: (l, 0, j)),
        out_shape=jax.ShapeDtypeStruct((DEPTH, MOD_ROWS, n), F32),
        compiler_params=_cparams(("arbitrary", "arbitrary")),
        name="modulation",
    )(cc, w_mod, b_mod.reshape(DEPTH, 1, n))


def _ada_norm(x, g, shift, scale):
    ms = jnp.mean(x * x, axis=-1, keepdims=True)
    return x * lax.rsqrt(ms + RMS_EPS) * g * (1.0 + scale) + shift


HALO = 8


def _batch_spec(tokens_per_batch, tm):
    return pl.BlockSpec((1, 1, D_MODEL), lambda i: ((i * tm) // tokens_per_batch, 0, 0))


def _in_kernel(x_ref, xp_ref, xn_ref, sh_ref, sc_ref, g_ref, cw_ref, cb_ref, *refs, segs, seq_len):
    ns = len(segs)
    w_refs, b_refs, o_refs = refs[:ns], refs[ns:2 * ns], refs[2 * ns:]
    tm = x_ref.shape[0]
    g, sh, sc = g_ref[...], sh_ref[0], sc_ref[0]
    hm = _ada_norm(x_ref[...], g, sh, sc)
    h = hm.astype(BF16)
    for (width, _, mode), w_ref, b_ref, o_ref in zip(segs, w_refs, b_refs, o_refs):
        step = min(width, 768)
        if mode == "conv":
            h_ext = jnp.concatenate([_ada_norm(xp_ref[...], g, sh, sc), hm, _ada_norm(xn_ref[...], g, sh, sc)],
                                    axis=0).astype(BF16)
            row = lax.broadcasted_iota(jnp.int32, (tm + 2 * HALO, step), 0)
            if seq_len >= tm:
                tpb = seq_len // tm
                pos = pl.program_id(0) % tpb
                no_prev = row == jnp.where(pos == 0, HALO, -1)
                no_next = row == jnp.where(pos == tpb - 1, HALO + tm - 1, -1)
            else:
                no_prev = functools.reduce(jnp.logical_or, [row == HALO + s for s in range(0, tm, seq_len)])
                no_next = functools.reduce(jnp.logical_or,
                                           [row == HALO + s + seq_len - 1 for s in range(0, tm, seq_len)])
        for c0 in range(0, width, step):
            cs = slice(c0, c0 + step)
            if mode == "conv":
                p = _dot(h_ext, w_ref[:, cs]) + b_ref[:, cs]
                prev = jnp.where(no_prev, 0.0, pltpu.roll(p, 1, 0))
                nxt = jnp.where(no_next, 0.0, pltpu.roll(p, tm + 2 * HALO - 1, 0))
                acc = cw_ref[0:1, cs] * prev + cw_ref[1:2, cs] * p + cw_ref[2:3, cs] * nxt + cb_ref[:, cs]
                acc = acc[HALO:HALO + tm]
            else:
                acc = _dot(h, w_ref[:, cs]) + b_ref[:, cs]
            o_ref[:, cs] = acc.astype(o_ref.dtype)


def _in_proj(x2d, shift, scale, g, conv_w, conv_b, weights, biases, segs, tokens_per_batch, tm):
    n_tok = x2d.shape[0]
    hb = tm // HALO
    last_halo = n_tok // HALO - 1
    in_specs = [
        pl.BlockSpec((tm, D_MODEL), lambda i: (i, 0)),
        pl.BlockSpec((HALO, D_MODEL), lambda i: (jnp.maximum(i * hb - 1, 0), 0)),
        pl.BlockSpec((HALO, D_MODEL), lambda i: (jnp.minimum((i + 1) * hb, last_halo), 0)),
        _batch_spec(tokens_per_batch, tm), _batch_spec(tokens_per_batch, tm),
        _resident((1, D_MODEL)),
        _resident(conv_w.shape), _resident(conv_b.shape),
    ]
    in_specs += [_resident((D_MODEL, s[0])) for s in segs]
    in_specs += [_resident((1, s[0])) for s in segs]
    return pl.pallas_call(
        functools.partial(_in_kernel, segs=segs, seq_len=tokens_per_batch),
        grid=(n_tok // tm,),
        in_specs=in_specs,
        out_specs=[pl.BlockSpec((tm, s[0]), lambda i: (i, 0)) for s in segs],
        out_shape=[jax.ShapeDtypeStruct((n_tok, s[0]), s[1]) for s in segs],
        compiler_params=_cparams(("parallel",)),
        name="in_proj",
    )(x2d, x2d, x2d, shift, scale, g, conv_w, conv_b, *weights, *biases)


def _hy_kernel(x1_ref, x2_ref, v_ref, skip_ref, f_ref, g_ref, k_ref, o_ref, z_ref, uf_ref, yf_ref, *, seq, blk):
    nb = seq // blk

    def long_conv(read_u, gate_ref, order, write):
        for j in range(nb):
            u_j = read_u(j).astype(BF16)
            for r0 in range(0, 2 * blk, blk):
                uf_ref[j, r0:r0 + blk, :] = _dot(f_ref[r0:r0 + blk, :], u_j).astype(BF16)
        for i in range(nb):
            for r0 in range(0, blk, FREQ_ROWS):
                rows, rows_im = slice(r0, r0 + FREQ_ROWS), slice(blk + r0, blk + r0 + FREQ_ROWS)
                re = im = None
                for j in range(nb):
                    lag = i - j + nb - 1
                    kre, kim, krx = (k_ref[order, lag, s, rows, :] for s in range(3))
                    ure, uim = uf_ref[j, rows, :], uf_ref[j, rows_im, :]
                    t_re = ure * kre - uim * kim
                    t_im = ure * kim + uim * krx
                    re = t_re if re is None else re + t_re
                    im = t_im if im is None else im + t_im
                yf_ref[i, rows, :] = re
                yf_ref[i, rows_im, :] = im
            y = _dot(g_ref[...], yf_ref[i])
            u = read_u(i).astype(F32)
            gate = gate_ref[0, i * blk:(i + 1) * blk, :].astype(F32)
            write(i, gate * (y + u * skip_ref[order:order + 1, :]))

    def write_z(i, val):
        z_ref[i * blk:(i + 1) * blk, :] = val

    def write_o(i, val):
        o_ref[0, i * blk:(i + 1) * blk, :] = val.astype(o_ref.dtype)

    long_conv(lambda j: v_ref[0, j * blk:(j + 1) * blk, :], x1_ref, 0, write_z)
    long_conv(lambda j: z_ref[j * blk:(j + 1) * blk, :], x2_ref, 1, write_o)


def _hyena(p_hy, skip, fmat, gmat, kspec, blk):
    bsz, seq, _ = p_hy.shape
    nb = seq // blk
    nlag = 2 * nb - 1
    nct = D_HY // HY_CT

    def col(off):
        return pl.BlockSpec((1, seq, HY_CT), lambda c, b, off=off: (b, 0, off + c))

    in_specs = [col(0), col(nct), col(2 * nct),
                pl.BlockSpec((2, HY_CT), lambda c, b: (0, c)),
                _resident((2 * blk, blk)), _resident((blk, 2 * blk)),
                pl.BlockSpec((2, nlag, 3, blk, HY_CT), lambda c, b: (0, 0, 0, 0, c),
                             pipeline_mode=pl.Buffered(1))]
    return pl.pallas_call(
        functools.partial(_hy_kernel, seq=seq, blk=blk),
        grid=(nct, bsz),
        in_specs=in_specs,
        out_specs=pl.BlockSpec((1, seq, HY_CT), lambda c, b: (b, 0, c)),
        out_shape=jax.ShapeDtypeStruct((bsz, seq, D_HY), BF16),
        scratch_shapes=[pltpu.VMEM((seq, HY_CT), F32), pltpu.VMEM((nb, 2 * blk, HY_CT), BF16),
                        pltpu.VMEM((nb, 2 * blk, HY_CT), BF16)],
        compiler_params=_cparams(("arbitrary", "arbitrary")),
        name="hyena",
    )(p_hy, p_hy, p_hy, skip, fmat, gmat, kspec)


def _dft_mats(blk):
    n = 2 * blk
    f = np.arange(blk)[:, None]
    t = np.arange(blk)[None, :]
    ang = ((f * t) % n) * (2.0 * math.pi / n)
    alt = np.where(t % 2 == 0, 1.0, -1.0)
    fre = np.cos(ang)
    fim = np.where(f == 0, alt, -np.sin(ang))
    fmat = np.concatenate([fre, fim], axis=0)
    gre = np.where(f == 0, 1.0 / n, (2.0 / n) * np.cos(ang)).T
    gim = np.where(f == 0, alt / n, -(2.0 / n) * np.sin(ang)).T
    gmat = np.concatenate([gre, gim], axis=1)
    return jnp.asarray(fmat.astype(BF16)), jnp.asarray(gmat.astype(BF16))


def _filter_kernel(z_ref, w1_ref, b1_ref, fr_ref, w2_ref, b2_ref, w3_ref, dl_ref, o_ref):
    z = z_ref[...]
    fr = fr_ref[0]
    a = jnp.sin(fr * (_dot3(z, w1_ref[0]) + b1_ref[0]))
    a = jnp.sin(fr * (_dot3(a, w2_ref[0]) + b2_ref[0]))
    h = _dot3(a, w3_ref[0])
    win = jnp.exp(-z[:, 0:1] * dl_ref[...])
    row = lax.broadcasted_iota(jnp.int32, win.shape, 0)
    win = jnp.where(row == jnp.where(pl.program_id(1) == 0, 0, -1), 0.0, win)
    for order in range(2):
        o_ref[0, order] = h[:, order * D_HY:(order + 1) * D_HY] * win


def _hyena_filters(seq, w1, b1, freq, w2, b2, w3):
    t = np.abs(np.arange(2 * seq, dtype=np.float64) - seq)[:, None]
    ang = (2.0 * math.pi / seq) * t * np.arange(1, HY_BANDS + 1, dtype=np.float64)[None, :]
    z = jnp.asarray(np.concatenate([t / seq, np.cos(ang), np.sin(ang),
                                    np.zeros((2 * seq, EMB_PAD - 1 - 2 * HY_BANDS))], axis=-1), dtype=F32)
    deltas = jnp.asarray(np.abs(np.linspace(math.log(HY_DECAY_TARGET) / HY_SLOW_DECAY,
                                            math.log(HY_DECAY_TARGET) / HY_FAST_DECAY, D_HY)).reshape(1, D_HY),
                         dtype=F32)
    w1p = jnp.pad(w1, ((0, 0), (0, EMB_PAD - w1.shape[1]), (0, 0)))
    fh = w1.shape[-1]
    rt = min(seq, 512)
    nneg = seq // rt

    def per_layer(shape):
        return pl.BlockSpec((1,) + shape, lambda l, r: (l,) + (0,) * len(shape))

    return pl.pallas_call(
        _filter_kernel,
        grid=(DEPTH, 2 * seq // rt),
        in_specs=[pl.BlockSpec((rt, EMB_PAD), lambda l, r: (r, 0)),
                  per_layer((EMB_PAD, fh)), per_layer((1, fh)), per_layer((1, fh)),
                  per_layer((fh, fh)), per_layer((1, fh)),
                  pl.BlockSpec((1, fh, 2 * D_HY), lambda l, r: (l, 0, jnp.where(r < nneg, 1, 0))),
                  pl.BlockSpec((1, D_HY), lambda l, r: (0, 0))],
        out_specs=pl.BlockSpec((1, 2, rt, D_HY), lambda l, r: (l, 0, r, 0)),
        out_shape=jax.ShapeDtypeStruct((DEPTH, 2, 2 * seq, D_HY), F32),
        compiler_params=_cparams(("arbitrary", "arbitrary")),
        name="hyena_filters",
    )(z, w1p, b1.reshape(DEPTH, 1, fh), freq.reshape(DEPTH, 1, fh), w2, b2.reshape(DEPTH, 1, fh), w3, deltas)


def _spec_kernel(kl_ref, kr_ref, fl_ref, fr_ref, o_ref):
    res = _dot(fl_ref[...], kl_ref[0, 0].astype(BF16)) + _dot(fr_ref[...], kr_ref[0, 0].astype(BF16))
    blk = res.shape[0] // 2
    re, im = res[:blk], res[blk:]
    first = lax.broadcasted_iota(jnp.int32, re.shape, 0) == 0
    o_ref[0, 0, 0, 0] = re.astype(o_ref.dtype)
    o_ref[0, 0, 0, 1] = jnp.where(first, 0.0, im).astype(o_ref.dtype)
    o_ref[0, 0, 0, 2] = jnp.where(first, im, re).astype(o_ref.dtype)


def _filter_spectra(kk, seq, blk):
    nb = seq // blk
    nlag = 2 * nb - 1
    n = 2 * blk
    f = np.arange(blk)[:, None]
    q = np.arange(n)[None, :]
    ang = ((f * q) % n) * (2.0 * math.pi / n)
    sgn = np.where(f % 2 == 0, 1.0, -1.0)
    alt = np.where(q % 2 == 0, 1.0, -1.0)
    fre = sgn * np.cos(ang)
    fim = np.where(f == 0, alt, -sgn * np.sin(ang))
    fre, fim = np.where(q == 0, 0.0, fre), np.where(q == 0, 0.0, fim)
    fk = jnp.asarray(np.concatenate([fre, fim], axis=0).astype(BF16))
    return pl.pallas_call(
        _spec_kernel,
        grid=(DEPTH, 2, nlag),
        in_specs=[pl.BlockSpec((1, 1, blk, D_HY), lambda l, o, m: (l, o, m, 0)),
                  pl.BlockSpec((1, 1, blk, D_HY), lambda l, o, m: (l, o, m + 1, 0)),
                  _resident((n, blk)), _resident((n, blk))],
        out_specs=pl.BlockSpec((1, 1, 1, 3, blk, D_HY), lambda l, o, m: (l, o, m, 0, 0, 0)),
        out_shape=jax.ShapeDtypeStruct((DEPTH, 2, nlag, 3, blk, D_HY), BF16),
        compiler_params=_cparams(("arbitrary", "arbitrary", "arbitrary")),
        name="filter_spectra",
    )(kk, kk, fk[:, :blk], fk[:, blk:])


def _alternate(*gens):
    live = list(gens)
    while live:
        for g in list(live):
            try:
                next(g)
            except StopIteration:
                live.remove(g)


def _gla_kernel(qkvr_ref, g_ref, wa_ref, ba_ref, nw_ref, s0_ref, tri_ref, y_ref, s_ref, o_ref, prep_ref, dec_ref,
                *, seq):
    nblk = seq // GLA_BLOCK
    nchunk = GLA_BLOCK // GLA_CHUNK
    lane = lax.broadcasted_iota(jnp.int32, (1, 2 * GLA_DK), 1)
    head_mask = [(lane < GLA_DK).astype(BF16), (lane >= GLA_DK).astype(BF16)]
    ri = lax.broadcasted_iota(jnp.int32, (GLA_BLOCK, GLA_BLOCK), 0)
    ci = lax.broadcasted_iota(jnp.int32, (GLA_BLOCK, GLA_BLOCK), 1)
    s_ref[...] = s0_ref[...]
    Q_LOC, Q_INT, K_ST, K_STRIP = 0, 1, 2, 3

    def block_rows(bi):
        return slice(bi * GLA_BLOCK, (bi + 1) * GLA_BLOCK)

    def prep(d, bi, buf):
        rows = block_rows(bi)
        g = g_ref[0, rows, :]
        g_hi = g.astype(BF16).astype(F32)
        packed = g_hi + pltpu.roll(g - g_hi, 2 * GLA_GATE_RANK, 1) + pltpu.roll(g_hi, 4 * GLA_GATE_RANK, 1)
        logit = _dot(packed.astype(BF16), wa_ref[d]) + ba_ref[d]
        yield
        la = (jnp.minimum(logit, 0.0) - jnp.log(1.0 + jnp.exp(-jnp.abs(logit)))) * (1.0 / GLA_GATE_NORM)
        la_hi, la_lo = _split(la)
        yield
        cum_blk = _dot(tri_ref[d], la_hi) + _dot(tri_ref[d], la_lo)
        zero = jnp.zeros((1, D_GLA_K), F32)
        if d == 0:
            ends = [cum_blk[(c + 1) * GLA_CHUNK - 1:(c + 1) * GLA_CHUNK] for c in range(nchunk)]
            bef = [zero] + ends[:-1]
        else:
            ends = [cum_blk[c * GLA_CHUNK:c * GLA_CHUNK + 1] for c in range(nchunk)]
            bef = ends[1:] + [zero]
        tot = [ends[c] - bef[c] for c in range(nchunk)]
        total = ends[nchunk - 1] if d == 0 else ends[0]
        cum_loc = jnp.concatenate(
            [cum_blk[c * GLA_CHUNK:(c + 1) * GLA_CHUNK] - bef[c] for c in range(nchunk)], axis=0)
        yield
        q_loc = qkvr_ref[0, rows, 0:D_GLA_K].astype(F32) * (GLA_DK ** -0.5) * jnp.exp(cum_loc)
        prep_ref[buf, d, Q_LOC] = q_loc.astype(BF16)
        yield
        k_inv = qkvr_ref[0, rows, D_GLA_K:2 * D_GLA_K].astype(F32) * jnp.exp(-cum_loc)
        yield
        dec_ref[buf, d, 0:1, :] = jnp.exp(total)
        for c in range(nchunk):
            rc = slice(c * GLA_CHUNK, (c + 1) * GLA_CHUNK)
            prep_ref[buf, d, Q_INT, rc, :] = (q_loc[rc] * jnp.exp(bef[c])).astype(BF16)
            k_end = k_inv[rc] * jnp.exp(tot[c])
            prep_ref[buf, d, K_ST, rc, :] = (k_end * jnp.exp(total - bef[c] - tot[c])).astype(BF16)
            for a in range(nchunk):
                earlier = c < a if d == 0 else c > a
                piece = k_end * jnp.exp(bef[a] - bef[c] - tot[c]) if earlier else k_inv[rc]
                prep_ref[buf, d, K_STRIP + a, rc, :] = piece.astype(BF16)
            yield

    def heads(d, bi, buf):
        rows = block_rows(bi)
        causal = (ci <= ri) if d == 0 else (ci >= ri)
        for h in range(GLA_HEADS):
            pair = slice((h // 2) * 2 * GLA_DK, (h // 2 + 1) * 2 * GLA_DK)
            hm = head_mask[h % 2]
            v_h = qkvr_ref[0, rows, 2 * D_GLA_K + h * GLA_DV:2 * D_GLA_K + (h + 1) * GLA_DV]
            strips = []
            for a in range(nchunk):
                rc = slice(a * GLA_CHUNK, (a + 1) * GLA_CHUNK)
                strips.append(_dot_nt(prep_ref[buf, d, Q_LOC, rc, pair] * hm, prep_ref[buf, d, K_STRIP + a, :, pair]))
            yield
            scores = jnp.where(causal, jnp.concatenate(strips, axis=0), 0.0).astype(BF16)
            st = s_ref[0, d, h]
            o_ref[d, rows, h * GLA_DV:(h + 1) * GLA_DV] = (
                _dot(scores, v_h) + _dot_nt(prep_ref[buf, d, Q_INT, :, pair] * hm, st.astype(BF16)))
            yield
            s_ref[0, d, h] = (dec_ref[buf, d, 0:1, pair] * st
                              + _dot_tn(v_h, prep_ref[buf, d, K_ST, :, pair] * hm))
            yield

    def finish(bi):
        rows = slice(bi * GLA_BLOCK, (bi + 1) * GLA_BLOCK)
        for h in range(GLA_HEADS):
            cols = slice(h * GLA_DV, (h + 1) * GLA_DV)
            o = o_ref[0, rows, cols] + o_ref[1, rows, cols]
            o = o * lax.rsqrt(jnp.mean(o * o, axis=-1, keepdims=True) + RMS_EPS) * nw_ref[...]
            r0 = 2 * D_GLA_K + D_GLA_V + h * GLA_DV
            r = qkvr_ref[0, rows, r0:r0 + GLA_DV].astype(F32)
            y_ref[0, rows, cols] = (o * (r * jax.nn.sigmoid(r))).astype(y_ref.dtype)
            yield

    ready = lambda it: [b for b in range(nblk) if max(b, nblk - 1 - b) == it]
    _alternate(prep(0, 0, 0), prep(1, nblk - 1, 0))
    for it in range(nblk):
        buf = it % 2
        streams = [heads(0, it, buf), heads(1, nblk - 1 - it, buf)]
        if it + 1 < nblk:
            streams += [prep(1, nblk - 2 - it, 1 - buf), prep(0, it + 1, 1 - buf)]
        streams += [finish(b) for b in ready(it - 1)]
        _alternate(*streams)
    _alternate(*[finish(b) for b in ready(nblk - 1)])


def _gla_consts():
    i = np.arange(GLA_BLOCK)
    lower = i[None, :] <= i[:, None]
    tri = np.stack([lower, lower.T])
    return jnp.asarray(tri.astype(np.float32), dtype=BF16)


def _gla(qkvr, rank, wa_pad, ba, norm_w, s0):
    bsz, seq, _ = qkvr.shape
    st_shape = (bsz, 2, GLA_HEADS, GLA_DV, 2 * GLA_DK)
    return pl.pallas_call(
        functools.partial(_gla_kernel, seq=seq),
        grid=(bsz,),
        in_specs=[
            pl.BlockSpec((1, seq, GLA_MAIN), lambda b: (b, 0, 0)),
            pl.BlockSpec((1, seq, RANK_PAD), lambda b: (b, 0, 0)),
            _resident((2, RANK_PAD, D_GLA_K)),
            _resident((2, 1, D_GLA_K)),
            _resident((1, GLA_DV)),
            pl.BlockSpec((1,) + st_shape[1:], lambda b: (b, 0, 0, 0, 0)),
            _resident((2, GLA_BLOCK, GLA_BLOCK)),
        ],
        out_specs=[pl.BlockSpec((1, seq, D_GLA_V), lambda b: (b, 0, 0)),
                   pl.BlockSpec((1,) + st_shape[1:], lambda b: (b, 0, 0, 0, 0))],
        out_shape=[jax.ShapeDtypeStruct((bsz, seq, D_GLA_V), BF16), jax.ShapeDtypeStruct(st_shape, F32)],
        scratch_shapes=[pltpu.VMEM((2, seq, D_GLA_V), F32),
                        pltpu.VMEM((2, 2, 3 + GLA_BLOCK // GLA_CHUNK, GLA_BLOCK, D_GLA_K), BF16),
                        pltpu.VMEM((2, 2, 8, D_GLA_K), F32)],
        compiler_params=_cparams(("parallel",)),
        name="gla",
    )(qkvr, rank, wa_pad, ba, norm_w, s0, _gla_consts())


def _pool_kernel(u_ref, a_ref, inv_ref, pw_ref, ps_ref, o_ref, *, seq, width):
    rows_n = seq // width
    nblk = seq // 256
    for g, w in enumerate(POOL_WINDOWS):
        cols = slice(g * POOL_GROUP, (g + 1) * POOL_GROUP)
        ug = u_ref[0, :, cols]
        col = jnp.concatenate([_dot(a_ref[g], ug[b * 256:(b + 1) * 256]) for b in range(nblk)], axis=0)
        col3 = col.reshape(rows_n, width, POOL_GROUP)
        acc = None
        for kk in range(w):
            s = kk - w // 2
            if abs(s) >= rows_n:
                continue
            if s == 0:
                term = col3
            elif s > 0:
                term = jnp.concatenate([col3[s:], jnp.zeros((s, width, POOL_GROUP), F32)], axis=0)
            else:
                term = jnp.concatenate([jnp.zeros((-s, width, POOL_GROUP), F32), col3[:s]], axis=0)
            acc = term if acc is None else acc + term
        mean = acc.reshape(seq, POOL_GROUP) * inv_ref[g]
        dlt = (mean - ug.astype(F32)).astype(BF16)
        o_ref[0, :, cols] = (_dot(dlt, pw_ref[g]) * ps_ref[:, cols]).astype(o_ref.dtype)


def _pool_consts(seq, width):
    rows_n = seq // width
    t = np.arange(256)
    tr, tc = t // width, t % width
    tt = np.arange(seq)
    row, colp = tt // width, tt % width
    mats, invs = [], []
    for w in POOL_WINDOWS:
        lo = np.clip(tc - w // 2, 0, width)
        hi = np.clip(tc - w // 2 + w, 0, width)
        m = (tr[:, None] == tr[None, :]) & (tc[None, :] >= lo[:, None]) & (tc[None, :] < hi[:, None])
        mats.append(m.astype(np.float32))
        cl, ch = np.clip(colp - w // 2, 0, width), np.clip(colp - w // 2 + w, 0, width)
        rl, rh = np.clip(row - w // 2, 0, rows_n), np.clip(row - w // 2 + w, 0, rows_n)
        cnt = ((rh - rl) * (ch - cl)).astype(np.float64)
        invs.append(np.broadcast_to((1.0 / cnt).astype(np.float32)[:, None], (seq, POOL_GROUP)))
    return jnp.asarray(np.stack(mats), dtype=BF16), jnp.asarray(np.stack(invs), dtype=F32)


def _pool(u, pool_w, pool_scale, width):
    bsz, seq, _ = u.shape
    amat, inv = _pool_consts(seq, width)
    ng = len(POOL_WINDOWS)
    return pl.pallas_call(
        functools.partial(_pool_kernel, seq=seq, width=width),
        grid=(bsz,),
        in_specs=[
            pl.BlockSpec((1, seq, D_POOL), lambda b: (b, 0, 0)),
            _resident((ng, 256, 256)),
            _resident((ng, seq, POOL_GROUP)),
            _resident((ng, POOL_GROUP, POOL_GROUP)),
            _resident((1, D_POOL)),
        ],
        out_specs=pl.BlockSpec((1, seq, D_POOL), lambda b: (b, 0, 0)),
        out_shape=jax.ShapeDtypeStruct((bsz, seq, D_POOL), BF16),
        compiler_params=_cparams(("parallel",)),
        name="pool",
    )(u, amat, inv, pool_w, pool_scale)


def _merge_kernel(yh_ref, yg_ref, yp_ref, x_ref, sh_ref, sc_ref, g1_ref, gm_ref, wgate_ref, bgate_ref,
                  wh_ref, wg_ref, wp_ref, wo_ref, o_ref):
    x = x_ref[...]
    h = _ada_norm(x, gm_ref[...], sh_ref[0], sc_ref[0]).astype(BF16)
    m = None
    for k, (y_ref, w_ref) in enumerate(((yh_ref, wh_ref), (yg_ref, wg_ref), (yp_ref, wp_ref))):
        cs = slice(k * D_MODEL, (k + 1) * D_MODEL)
        gate = jax.nn.sigmoid(_dot(h, wgate_ref[:, cs]) + bgate_ref[:, cs])
        term = gate * _dot(y_ref[...], w_ref[...])
        m = term if m is None else m + term
    o_ref[...] = x + g1_ref[0] * _dot(m.astype(BF16), wo_ref[...])


def _merge(y_hy, y_gla, y_pool, x2d, shift, scale, g1, gm, w_gate, b_gate, w_hy, w_gla, w_pool, w_out,
           tokens_per_batch, tm):
    n_tok = x2d.shape[0]

    def tok(width):
        return pl.BlockSpec((tm, width), lambda i: (i, 0))

    def per_batch():
        return _batch_spec(tokens_per_batch, tm)

    return pl.pallas_call(
        _merge_kernel,
        grid=(n_tok // tm,),
        in_specs=[tok(D_HY), tok(D_GLA_V), tok(D_POOL), tok(D_MODEL), per_batch(), per_batch(), per_batch(),
                  _resident((1, D_MODEL)), _resident((D_MODEL, 3 * D_MODEL)), _resident((1, 3 * D_MODEL)),
                  _resident((D_HY, D_MODEL)), _resident((D_GLA_V, D_MODEL)), _resident((D_POOL, D_MODEL)),
                  _resident((D_MODEL, D_MODEL))],
        out_specs=tok(D_MODEL),
        out_shape=jax.ShapeDtypeStruct((n_tok, D_MODEL), F32),
        compiler_params=_cparams(("parallel",)),
        name="merge",
    )(y_hy, y_gla, y_pool, x2d, shift, scale, g1, gm, w_gate, b_gate, w_hy, w_gla, w_pool, w_out)


def _mlp_kernel(x_ref, sh_ref, sc_ref, g2_ref, gn_ref, wu_ref, wd_ref, nf_ref, o_ref, *, final_norm):
    x = x_ref[...]
    h = _ada_norm(x, gn_ref[...], sh_ref[0], sc_ref[0]).astype(BF16)
    step = 1024
    acc = jnp.zeros(x.shape, F32)
    for c0 in range(0, D_FF, step):
        u = jnp.maximum(_dot(h, wu_ref[:, c0:c0 + step]), 0.0)
        acc = acc + _dot((u * u).astype(BF16), wd_ref[c0:c0 + step, :])
    y = x + g2_ref[0] * acc
    if final_norm:
        y = y * lax.rsqrt(jnp.mean(y * y, axis=-1, keepdims=True) + RMS_EPS) * nf_ref[...]
    o_ref[...] = y


def _mlp(x2d, shift, scale, g2, gn, w_up, w_down, norm_final, tokens_per_batch, tm, final_norm):
    n_tok = x2d.shape[0]

    def per_batch():
        return _batch_spec(tokens_per_batch, tm)

    return pl.pallas_call(
        functools.partial(_mlp_kernel, final_norm=final_norm),
        grid=(n_tok // tm,),
        in_specs=[pl.BlockSpec((tm, D_MODEL), lambda i: (i, 0)), per_batch(), per_batch(), per_batch(),
                  _resident((1, D_MODEL)), _resident((D_MODEL, D_FF)), _resident((D_FF, D_MODEL)),
                  _resident((1, D_MODEL))],
        out_specs=pl.BlockSpec((tm, D_MODEL), lambda i: (i, 0)),
        out_shape=jax.ShapeDtypeStruct((n_tok, D_MODEL), F32),
        compiler_params=_cparams(("parallel",)),
        name="mlp",
    )(x2d, shift, scale, g2, gn, w_up, w_down, norm_final)


SEG_HY = (HY_COLS, BF16, "conv")
SEG_GLA = (GLA_MAIN, BF16, "plain")
SEG_RANK = (RANK_PAD, F32, "plain")
SEG_POOL = (D_POOL, BF16, "plain")


def _hy_block(seq):
    return 512 if seq % 512 == 0 and seq >= 2048 else 256


def _token_tile(seq):
    for tm in (1024, 512):
        if seq % tm == 0:
            return tm
    return 256


def _mixers(pieces, lp, seq, width, s0):
    p_gla, p_hy, p_rank, p_pool = pieces
    n_tok = p_hy.shape[0]
    bsz = n_tok // seq
    blk = _hy_block(seq)
    fmat, gmat = _dft_mats(blk)
    y_hy = _hyena(p_hy.reshape(bsz, seq, HY_COLS), lp["hy_skip"], fmat, gmat, lp["kspec"][seq], blk)
    y_gla, s_fin = _gla(p_gla.reshape(bsz, seq, GLA_MAIN), p_rank.reshape(bsz, seq, RANK_PAD),
                        lp["wa_pad"], lp["ba"], lp["gla_norm_w"], s0)
    y_pool = _pool(p_pool.reshape(bsz, seq, D_POOL), lp["pool_w"], lp["pool_scale"], width)
    return (y_hy.reshape(n_tok, D_HY), y_gla.reshape(n_tok, D_GLA_V), y_pool.reshape(n_tok, D_POOL)), s_fin


def kernel(x, c, ctx, c_ctx, w_mod, b_mod, norm_mix, norm_ffn, w_in, b_in, hy_short_w, hy_short_b, hy_f_w1, hy_f_b1, hy_f_freq, hy_f_w2, hy_f_b2, hy_f_w3, hy_skip, gla_wa_f, gla_ba_f, gla_wa_b, gla_ba_b, gla_norm_w, pool_w, pool_scale, w_br_hy, w_br_gla, w_br_pool, w_out, w_up, w_down, norm_final):
    bsz, seq, _ = x.shape
    ctx_len = ctx.shape[1]
    tm_x, tm_c = _token_tile(seq), _token_tile(bsz * ctx_len)
    assert tm_c % ctx_len == 0 or ctx_len % tm_c == 0

    cc = jnp.concatenate([c, c_ctx[None, :], jnp.zeros((MOD_ROWS - bsz - 1, D_MODEL), F32)], axis=0)
    mod = _modulation(cc, w_mod, b_mod)

    x2d = x.reshape(bsz * seq, D_MODEL)
    c2d = ctx.reshape(bsz * ctx_len, D_MODEL)
    s_zero = jnp.zeros((bsz, 2, GLA_HEADS, GLA_DV, 2 * GLA_DK), F32)
    nf = norm_final.reshape(1, D_MODEL)
    filt_args = (hy_f_w1, hy_f_b1, hy_f_freq, hy_f_w2, hy_f_b2, hy_f_w3)
    kspec_x = _filter_spectra(_hyena_filters(seq, *filt_args), seq, _hy_block(seq))
    kspec_c = _filter_spectra(_hyena_filters(ctx_len, *filt_args), ctx_len, _hy_block(ctx_len))

    for l in range(DEPTH):
        mx = [mod[l, :bsz, i * D_MODEL:(i + 1) * D_MODEL].reshape(bsz, 1, D_MODEL) for i in range(6)]
        mc = [jnp.broadcast_to(mod[l, bsz:bsz + 1, i * D_MODEL:(i + 1) * D_MODEL].reshape(1, 1, D_MODEL),
                               (bsz, 1, D_MODEL)) for i in range(6)]
        wl, bl = w_in[l], b_in[l]
        w_rank = jnp.pad(wl[:, RANK_OFF:POOL_OFF], ((0, 0), (0, RANK_PAD - 2 * GLA_GATE_RANK)))
        b_rank = jnp.pad(bl[RANK_OFF:POOL_OFF], (0, RANK_PAD - 2 * GLA_GATE_RANK))
        seg_w = {"hy": wl[:, :GLA_OFF], "gla": wl[:, GLA_OFF:RANK_OFF], "rank": w_rank,
                 "pool": wl[:, POOL_OFF:GATE_OFF], "gate": wl[:, GATE_OFF:]}
        seg_b = {"hy": bl[:GLA_OFF], "gla": bl[GLA_OFF:RANK_OFF], "rank": b_rank,
                 "pool": bl[POOL_OFF:GATE_OFF], "gate": bl[GATE_OFF:]}
        seg_w = {k: v.astype(BF16) for k, v in seg_w.items()}
        seg_b = {k: v.reshape(1, -1) for k, v in seg_b.items()}

        zr = jnp.zeros((GLA_GATE_RANK, D_GLA_K), F32)
        wa = jnp.stack([jnp.concatenate([gla_wa_f[l], zr]), jnp.concatenate([zr, gla_wa_b[l]])])
        wa_hi = wa.astype(BF16)
        wa_lo = (wa - wa_hi.astype(F32)).astype(BF16)
        wa_pad = jnp.concatenate([wa_hi, wa_hi, wa_lo, jnp.zeros((2, RANK_PAD - 6 * GLA_GATE_RANK, D_GLA_K), BF16)],
                                 axis=1)
        lp = {
            "hy_skip": hy_skip[l],
            "wa_pad": wa_pad, "ba": jnp.stack([gla_ba_f[l], gla_ba_b[l]]).reshape(2, 1, D_GLA_K),
            "gla_norm_w": gla_norm_w[l].reshape(1, GLA_DV),
            "pool_w": pool_w[l].astype(BF16), "pool_scale": pool_scale[l].reshape(1, D_POOL),
            "kspec": {seq: kspec_x[l], ctx_len: kspec_c[l]},
        }
        cw, cb = hy_short_w[l], hy_short_b[l].reshape(1, HY_COLS)
        gm = norm_mix[l].reshape(1, D_MODEL)
        gn = norm_ffn[l].reshape(1, D_MODEL)
        wbh, wbg, wbp = w_br_hy[l].astype(BF16), w_br_gla[l].astype(BF16), w_br_pool[l].astype(BF16)
        wo, wu, wd = w_out[l].astype(BF16), w_up[l].astype(BF16), w_down[l].astype(BF16)
        names = ("gla", "hy", "rank", "pool")
        segs = (SEG_GLA, SEG_HY, SEG_RANK, SEG_POOL)

        if l == DEPTH - 1:
            pc_gla, pc_rank = _in_proj(c2d, mc[0], mc[1], gm, cw, cb, [seg_w["gla"], seg_w["rank"]],
                                       [seg_b["gla"], seg_b["rank"]], (SEG_GLA, SEG_RANK), ctx_len, tm_c)
            _, s_ctx = _gla(pc_gla.reshape(bsz, ctx_len, GLA_MAIN), pc_rank.reshape(bsz, ctx_len, RANK_PAD),
                            lp["wa_pad"], lp["ba"], lp["gla_norm_w"], s_zero)
        else:
            pc = _in_proj(c2d, mc[0], mc[1], gm, cw, cb, [seg_w[n] for n in names], [seg_b[n] for n in names],
                          segs, ctx_len, tm_c)
            ys, s_ctx = _mixers(pc, lp, ctx_len, ctx_len, s_zero)
            c2d = _merge(*ys, c2d, mc[0], mc[1], mc[2], gm, seg_w["gate"], seg_b["gate"], wbh, wbg, wbp, wo,
                         ctx_len, tm_c)
            c2d = _mlp(c2d, mc[3], mc[4], mc[5], gn, wu, wd, nf, ctx_len, tm_c, False)

        px = _in_proj(x2d, mx[0], mx[1], gm, cw, cb, [seg_w[n] for n in names], [seg_b[n] for n in names],
                      segs, seq, tm_x)
        ys, _ = _mixers(px, lp, seq, GRID_W, s_ctx)
        x2d = _merge(*ys, x2d, mx[0], mx[1], mx[2], gm, seg_w["gate"], seg_b["gate"], wbh, wbg, wbp, wo, seq, tm_x)
        x2d = _mlp(x2d, mx[3], mx[4], mx[5], gn, wu, wd, nf, seq, tm_x, l == DEPTH - 1)
    return x2d.reshape(bsz, seq, D_MODEL)
```
